```python
import math
import jax, jax.numpy as jnp
from jax import lax
import numpy as np

D_MODEL = 1024
BATCH = 8
SEQ = 4096
DEPTH = 1

HEAD_DIM = 64
FOX_HEADS = 8
SWA_HEADS = 8
SWA_KV_HEADS = 2
SWA_GROUP = SWA_HEADS // SWA_KV_HEADS
MIX_WIDTH = (FOX_HEADS + SWA_HEADS) * HEAD_DIM
WINDOW = 128
BLOCK = 128
N_BUCKETS = 32
MAX_DISTANCE = 128
N_EXPERTS = 32
TOP_K = 4
D_EXPERT = D_MODEL
SWIGLU_LIMIT = 7.0
SWIGLU_ALPHA = 1.702
EXPERT_BLOCK = 128
EPS = 1e-6
IN_SPLITS = (FOX_HEADS * HEAD_DIM, FOX_HEADS * HEAD_DIM, FOX_HEADS * HEAD_DIM, FOX_HEADS,
             SWA_HEADS * HEAD_DIM, SWA_KV_HEADS * HEAD_DIM, SWA_KV_HEADS * HEAD_DIM)
IN_WIDTH = sum(IN_SPLITS)

kernel_name = "hymba_fox_swa_sink_moe_layer"


def rms_norm(x, g):
    xf = x.astype(jnp.float32)
    y = xf * lax.rsqrt(jnp.mean(xf * xf, axis=-1, keepdims=True) + EPS)
    return (y * g.astype(jnp.float32)).astype(x.dtype)


def t5_bucket(n):
    max_exact = N_BUCKETS // 2
    nf = jnp.maximum(n, 1).astype(jnp.float32)
    large = max_exact + (jnp.log(nf / max_exact) / math.log(MAX_DISTANCE / max_exact)
                         * (N_BUCKETS - max_exact)).astype(jnp.int32)
    large = jnp.minimum(large, N_BUCKETS - 1)
    return jnp.where(n < max_exact, n, large)


def fox_attention(q, k, v, f_logit):
    b, s, h, dh = q.shape
    nb = s // BLOCK
    c = jnp.cumsum(jax.nn.log_sigmoid(f_logit.astype(jnp.float32)), axis=1).transpose(0, 2, 1)
    scale = dh ** -0.5
    kpos = jnp.arange(s)
    q_blocks = q.reshape(b, nb, BLOCK, h, dh).transpose(1, 0, 3, 2, 4)
    c_blocks = c.reshape(b, h, nb, BLOCK).transpose(2, 0, 1, 3)
    starts = jnp.arange(nb) * BLOCK

    def one_block(args):
        qb, cb, start = args
        logits = jnp.einsum('bhqd,bshd->bhqs', qb, k).astype(jnp.float32) * scale
        logits = logits + cb[..., None] - c[:, :, None, :]
        qpos = start + jnp.arange(BLOCK)
        causal = kpos[None, :] <= qpos[:, None]
        logits = jnp.where(causal, logits, -jnp.inf)
        p = jax.nn.softmax(logits, axis=-1).astype(v.dtype)
        return jnp.einsum('bhqs,bshd->bqhd', p, v)

    out = lax.map(one_block, (q_blocks, c_blocks, starts))
    return out.transpose(1, 0, 2, 3, 4).reshape(b, s, h * dh)


def swa_sink_attention(q, k, v, sinks, rel_bias):
    b, s, hq, dh = q.shape
    nb = s // BLOCK
    qb = q.reshape(b, nb, BLOCK, SWA_KV_HEADS, SWA_GROUP, dh)

    def band(t):
        tb = t.reshape(b, nb, BLOCK, SWA_KV_HEADS, dh)
        prev = jnp.pad(tb[:, :-1], ((0, 0), (1, 0), (0, 0), (0, 0), (0, 0)))
        return jnp.concatenate([prev, tb], axis=2)

    kw, vw = band(k), band(v)
    logits = jnp.einsum('bnqhgd,bnshd->bnhgqs', qb, kw).astype(jnp.float32) * dh ** -0.5
    qi = jnp.arange(BLOCK)[:, None]
    kj = jnp.arange(2 * BLOCK)[None, :]
    dist = qi + BLOCK - kj
    bias = rel_bias.astype(jnp.float32)[t5_bucket(jnp.maximum(dist, 0))]
    bias = bias.transpose(2, 0, 1).reshape(SWA_KV_HEADS, SWA_GROUP, BLOCK, 2 * BLOCK)
    key_pos = jnp.arange(nb)[:, None, None] * BLOCK - BLOCK + kj[None]
    valid = (dist >= 0) & (dist < WINDOW) & (key_pos >= 0)
    logits = jnp.where(valid[None, :, None, None], logits + bias, -jnp.inf)
    sink = jnp.broadcast_to(sinks.astype(jnp.float32).reshape(SWA_KV_HEADS, SWA_GROUP, 1, 1),
                            logits.shape[:-1] + (1,))
    probs = jax.nn.softmax(jnp.concatenate([logits, sink], axis=-1), axis=-1)[..., :-1].astype(v.dtype)
    out = jnp.einsum('bnhgqs,bnshd->bnqhgd', probs, vw)
    return out.reshape(b, s, hq * dh)


def moe_ffn(x, w_router, b_router, w_up_gate, b_up_gate, w_down, b_down):
    b, s, d = x.shape
    n = b * s
    xt = x.reshape(n, d)
    logits = (xt @ w_router).astype(jnp.float32) + b_router.astype(jnp.float32)
    top_val, top_idx = lax.top_k(logits, TOP_K)
    gates = jax.nn.softmax(top_val, axis=-1)
    n_assign = n * TOP_K
    e_flat = top_idx.reshape(-1).astype(jnp.int32)
    tok_flat = jnp.arange(n_assign, dtype=jnp.int32) // TOP_K
    g_flat = gates.reshape(-1)
    order = jnp.argsort(e_flat)
    e_sorted = e_flat[order]
    counts = jnp.zeros((N_EXPERTS,), jnp.int32).at[e_flat].add(1)
    padded = (counts + EXPERT_BLOCK - 1) // EXPERT_BLOCK * EXPERT_BLOCK
    start = jnp.cumsum(counts) - counts
    pstart = jnp.cumsum(padded) - padded
    pend = pstart + padded
    dest = pstart[e_sorted] + jnp.arange(n_assign, dtype=jnp.int32) - start[e_sorted]
    rows = n_assign + N_EXPERTS * EXPERT_BLOCK
    row_tok = jnp.full((rows,), n, jnp.int32).at[dest].set(tok_flat[order])
    row_gate = jnp.zeros((rows,), jnp.float32).at[dest].set(g_flat[order])
    nblk = rows // EXPERT_BLOCK
    blk_start = jnp.arange(nblk, dtype=jnp.int32) * EXPERT_BLOCK
    blk_exp = jnp.minimum(jnp.sum(pend[None, :] <= blk_start[:, None], axis=1), N_EXPERTS - 1)
    x_pad = jnp.concatenate([xt, jnp.zeros((1, d), xt.dtype)], axis=0)
    x_rows = x_pad[row_tok].reshape(nblk, EXPERT_BLOCK, d)

    def expert_block(args):
        xb, e = args
        gu = xb @ w_up_gate[e] + b_up_gate[e]
        gate = jnp.minimum(gu[:, :D_EXPERT], SWIGLU_LIMIT)
        up = jnp.clip(gu[:, D_EXPERT:], -SWIGLU_LIMIT, SWIGLU_LIMIT)
        act = (up + 1) * (gate * jax.nn.sigmoid(SWIGLU_ALPHA * gate))
        return act @ w_down[e] + b_down[e]

    y_rows = lax.map(expert_block, (x_rows, blk_exp)).reshape(rows, d)
    y = jax.ops.segment_sum(y_rows * row_gate[:, None].astype(y_rows.dtype), row_tok,
                            num_segments=n + 1)[:n]
    return y.reshape(b, s, d)


def setup_inputs(seed: int = 0) -> dict:
    key = jax.random.key(seed)
    ks = jax.random.split(key, 20)
    f32 = jnp.float32
    nrm = lambda k, shape, scale: jax.random.normal(k, shape, f32) * scale
    return {
        "x": nrm(ks[0], (BATCH, SEQ, D_MODEL), 1.0),
        "norm_mix": 1.0 + nrm(ks[1], (DEPTH, D_MODEL), 0.02),
        "w_in": nrm(ks[2], (DEPTH, D_MODEL, IN_WIDTH), D_MODEL ** -0.5),
        "b_forget": 2.0 + nrm(ks[3], (DEPTH, FOX_HEADS), 0.5),
        "q_norm_fox": 1.0 + nrm(ks[4], (DEPTH, HEAD_DIM), 0.02),
        "k_norm_fox": 1.0 + nrm(ks[5], (DEPTH, HEAD_DIM), 0.02),
        "q_norm_swa": 1.0 + nrm(ks[6], (DEPTH, HEAD_DIM), 0.02),
        "k_norm_swa": 1.0 + nrm(ks[7], (DEPTH, HEAD_DIM), 0.02),
        "sinks": nrm(ks[8], (DEPTH, SWA_HEADS), 0.5),
        "rel_bias": nrm(ks[9], (N_BUCKETS, SWA_HEADS), 0.5),
        "w_out": nrm(ks[10], (DEPTH, MIX_WIDTH, D_MODEL), MIX_WIDTH ** -0.5),
        "norm_moe": 1.0 + nrm(ks[11], (DEPTH, D_MODEL), 0.02),
        "w_router": nrm(ks[12], (DEPTH, D_MODEL, N_EXPERTS), D_MODEL ** -0.5),
        "b_router": nrm(ks[13], (DEPTH, N_EXPERTS), 0.01),
        "w_up_gate": nrm(ks[14], (DEPTH, N_EXPERTS, D_MODEL, 2 * D_EXPERT), D_MODEL ** -0.5),
        "b_up_gate": nrm(ks[15], (DEPTH, N_EXPERTS, 2 * D_EXPERT), 0.02),
        "w_down": nrm(ks[16], (DEPTH, N_EXPERTS, D_EXPERT, D_MODEL), D_EXPERT ** -0.5),
        "b_down": nrm(ks[17], (DEPTH, N_EXPERTS, D_MODEL), 0.02),
    }


def reference(x, norm_mix, w_in, b_forget, q_norm_fox, k_norm_fox, q_norm_swa, k_norm_swa,
              sinks, rel_bias, w_out, norm_moe, w_router, b_router, w_up_gate, b_up_gate,
              w_down, b_down):
    b, s, _ = x.shape
    points = [int(p) for p in np.cumsum(IN_SPLITS)[:-1]]
    h = x
    for l in range(DEPTH):
        hn = rms_norm(h, norm_mix[l])
        proj = hn @ w_in[l]
        qa, ka, va, fa, qb, kb, vb = jnp.split(proj, points, axis=-1)
        qa = rms_norm(qa.reshape(b, s, FOX_HEADS, HEAD_DIM), q_norm_fox[l])
        ka = rms_norm(ka.reshape(b, s, FOX_HEADS, HEAD_DIM), k_norm_fox[l])
        va = va.reshape(b, s, FOX_HEADS, HEAD_DIM)
        out_a = fox_attention(qa, ka, va, fa + b_forget[l])
        qb = rms_norm(qb.reshape(b, s, SWA_HEADS, HEAD_DIM), q_norm_swa[l])
        kb = rms_norm(kb.reshape(b, s, SWA_KV_HEADS, HEAD_DIM), k_norm_swa[l])
        vb = vb.reshape(b, s, SWA_KV_HEADS, HEAD_DIM)
        out_b = swa_sink_attention(qb, kb, vb, sinks[l], rel_bias)
        h = h + jnp.concatenate([out_a, out_b], axis=-1) @ w_out[l]
        h = h + moe_ffn(rms_norm(h, norm_moe[l]), w_router[l], b_router[l], w_up_gate[l],
                        b_up_gate[l], w_down[l], b_down[l])
    return h
```

```python
import functools
import math

import numpy as np
import jax
import jax.numpy as jnp
from jax import lax
from jax.experimental import pallas as pl
from jax.experimental.pallas import tpu as pltpu

HEAD_DIM = 64
FOX_HEADS = 8
SWA_HEADS = 8
SWA_KV_HEADS = 2
SWA_GROUP = SWA_HEADS // SWA_KV_HEADS
WINDOW = 128
N_BUCKETS = 32
MAX_DISTANCE = 128
TOP_K = 4
SWIGLU_LIMIT = 7.0
SWIGLU_ALPHA = 1.702
EPS = 1e-6

LANES = 128
VMEM_LIMIT_BYTES = 56 * 1024 * 1024
NEG_INF = float("-inf")

F32 = jnp.float32
BF16 = jnp.bfloat16


def _lane_iota(shape):
    return lax.broadcasted_iota(jnp.int32, shape, len(shape) - 1)


def _split3_bf16(v):
    hi = v.astype(BF16)
    r1 = v - hi.astype(F32)
    mid = r1.astype(BF16)
    lo = (r1 - mid.astype(F32)).astype(BF16)
    return hi, mid, lo


def _pack_bf16_pairs(v):
    w = v.shape[1] // 2
    u = pltpu.bitcast(v.astype(BF16).astype(F32), jnp.uint32)
    return u[:, :w] | (u[:, w:] >> 16)


def _unpack_bf16_pairs(u):
    first = pltpu.bitcast(u & jnp.uint32(0xFFFF0000), F32)
    second = pltpu.bitcast(u << 16, F32)
    return first, second


CUM_CHUNK = 128


def _proj_kernel(x_ref, g_ref, wqa_ref, wka_ref, wrest_ref, bf_ref, gqa_ref, gka_ref, gsq_ref, gsk_ref, tri_ref,
                 qa_ref, ka_ref, fv_ref, sq_ref, sk_ref, sv_ref, carry_ref, *, tiles_per_seq):
    i = pl.program_id(0)
    tm = x_ref.shape[0]
    nh = FOX_HEADS
    scale = HEAD_DIM ** -0.5

    @pl.when(i % tiles_per_seq == 0)
    def _():
        carry_ref[...] = jnp.zeros_like(carry_ref)

    x = x_ref[...]
    ms = jnp.mean(x * x, axis=-1, keepdims=True)
    hn = (x * lax.rsqrt(ms + EPS) * g_ref[...]).astype(BF16)

    qa = jnp.dot(hn, wqa_ref[...], preferred_element_type=F32)
    ka = jnp.dot(hn, wka_ref[...], preferred_element_type=F32)
    rest = jnp.dot(hn, wrest_ref[...], preferred_element_type=F32)

    lane_w = _lane_iota((tm, nh * LANES)) % LANES
    upper = lane_w >= HEAD_DIM

    fl = qa + bf_ref[...]
    ls = jnp.minimum(fl, 0.0) - jnp.log(1.0 + jnp.exp(-jnp.abs(fl)))
    ls = jnp.where(upper, ls, 0.0)
    tri = tri_ref[...]
    carry = carry_ref[0:1, :]
    c_chunks = []
    for r in range(tm // CUM_CHUNK):
        hi, mid, lo = _split3_bf16(ls[r * CUM_CHUNK:(r + 1) * CUM_CHUNK, :])
        cs = jnp.dot(tri, jnp.concatenate([hi, mid, lo], axis=0), preferred_element_type=F32) + carry
        carry = cs[CUM_CHUNK - 1:CUM_CHUNK, :]
        c_chunks.append(cs)
    carry_ref[0:1, :] = carry
    c = jnp.concatenate(c_chunks, axis=0)

    c1 = c.astype(BF16).astype(F32)
    c2 = (c - c1).astype(BF16).astype(F32)
    c3 = c - c1 - c2

    def head_norm(t, g_row):
        outs = []
        for h in range(t.shape[1] // LANES):
            th = t[:, h * LANES:(h + 1) * LANES]
            lo_mask = _lane_iota(th.shape) < HEAD_DIM
            ss = jnp.sum(jnp.where(lo_mask, th * th, 0.0), axis=-1, keepdims=True)
            outs.append(th * lax.rsqrt(ss * (1.0 / HEAD_DIM) + EPS))
        return jnp.concatenate(outs, axis=1) * g_row

    qn = head_norm(qa, gqa_ref[...]) * scale
    kn = head_norm(ka, gka_ref[...])

    one = jnp.float32(1.0)
    zero = jnp.float32(0.0)
    q_aug = jnp.where(lane_w == 64, c1, jnp.where(lane_w == 65, c2, jnp.where(lane_w == 66, c3,
                      jnp.where(lane_w < 70, one, zero))))
    k_aug = jnp.where(lane_w < 67, one, jnp.where(lane_w == 67, -c1, jnp.where(lane_w == 68, -c2,
                      jnp.where(lane_w == 69, -c3, zero))))
    qa_ref[...] = jnp.where(upper, q_aug, qn).astype(BF16)
    ka_ref[...] = jnp.where(upper, k_aug, kn).astype(BF16)

    nfv = FOX_HEADS * HEAD_DIM
    nsq = SWA_HEADS * HEAD_DIM
    nsk = SWA_KV_HEADS * HEAD_DIM
    fv_ref[...] = rest[:, :nfv].astype(BF16)

    def half_norm(t, g_row):
        outs = []
        for j in range(t.shape[1] // LANES):
            tj = t[:, j * LANES:(j + 1) * LANES]
            lo_mask = _lane_iota(tj.shape) < HEAD_DIM
            sq = tj * tj
            s_lo = jnp.sum(jnp.where(lo_mask, sq, 0.0), axis=-1, keepdims=True)
            s_hi = jnp.sum(jnp.where(lo_mask, 0.0, sq), axis=-1, keepdims=True)
            r = jnp.where(lo_mask, lax.rsqrt(s_lo * (1.0 / HEAD_DIM) + EPS), lax.rsqrt(s_hi * (1.0 / HEAD_DIM) + EPS))
            outs.append(tj * r)
        return jnp.concatenate(outs, axis=1) * g_row

    sq_ref[...] = (half_norm(rest[:, nfv:nfv + nsq], gsq_ref[...]) * scale).astype(BF16)
    sk_ref[...] = half_norm(rest[:, nfv + nsq:nfv + nsq + nsk], gsk_ref[...]).astype(BF16)
    sv_ref[...] = rest[:, nfv + nsq + nsk:].astype(BF16)


def _proj(x2d, norm_mix, w_in, b_forget, q_norm_fox, k_norm_fox, q_norm_swa, k_norm_swa, seq):
    n, d = x2d.shape
    nh, dh = FOX_HEADS, HEAD_DIM
    tm = min(512, seq)
    assert seq % tm == 0 and tm % CUM_CHUNK == 0
    points = np.cumsum([nh * dh, nh * dh, nh * dh, nh, SWA_HEADS * dh, SWA_KV_HEADS * dh, SWA_KV_HEADS * dh])
    wfq, wfk, wfv, wff, wsq, wsk, wsv = jnp.split(w_in, [int(p) for p in points[:-1]], axis=1)

    wfq = wfq.reshape(d, nh, dh)
    wff_rep = jnp.broadcast_to(wff[:, :, None], (d, nh, dh))
    wqa = jnp.concatenate([wfq, wff_rep], axis=2).reshape(d, nh * LANES).astype(BF16)
    wka = jnp.concatenate([wfk.reshape(d, nh, dh), jnp.zeros((d, nh, dh), w_in.dtype)], axis=2)
    wka = wka.reshape(d, nh * LANES).astype(BF16)
    wsq = wsq.reshape(d, SWA_KV_HEADS, SWA_GROUP, dh).transpose(0, 2, 1, 3).reshape(d, SWA_HEADS * dh)
    wrest = jnp.concatenate([wfv, wsq, wsk, wsv], axis=1).astype(BF16)

    zeros_h = jnp.zeros((nh, dh), F32)
    bf_row = jnp.concatenate([zeros_h, jnp.broadcast_to(b_forget[:, None], (nh, dh))], axis=1).reshape(1, nh * LANES)
    gqa = jnp.concatenate([jnp.broadcast_to(q_norm_fox[None], (nh, dh)), zeros_h], axis=1).reshape(1, nh * LANES)
    gka = jnp.concatenate([jnp.broadcast_to(k_norm_fox[None], (nh, dh)), zeros_h], axis=1).reshape(1, nh * LANES)
    gsq = jnp.tile(q_norm_swa, SWA_HEADS).reshape(1, SWA_HEADS * dh)
    gsk = jnp.tile(k_norm_swa, SWA_KV_HEADS).reshape(1, SWA_KV_HEADS * dh)
    tri = np.tril(np.ones((CUM_CHUNK, CUM_CHUNK), np.float32))
    tri3 = jnp.asarray(np.concatenate([tri, tri, tri], axis=1), BF16)

    nrest = wrest.shape[1]
    const = lambda i: (0, 0)
    row = lambda i: (i, 0)
    out_shapes = (
        jax.ShapeDtypeStruct((n, nh * LANES), BF16),
        jax.ShapeDtypeStruct((n, nh * LANES), BF16),
        jax.ShapeDtypeStruct((n, nh * dh), BF16),
        jax.ShapeDtypeStruct((n, SWA_HEADS * dh), BF16),
        jax.ShapeDtypeStruct((n, SWA_KV_HEADS * dh), BF16),
        jax.ShapeDtypeStruct((n, SWA_KV_HEADS * dh), BF16),
    )
    return pl.pallas_call(
        functools.partial(_proj_kernel, tiles_per_seq=seq // tm),
        grid=(n // tm,),
        in_specs=[
            pl.BlockSpec((tm, d), row),
            pl.BlockSpec((1, d), const),
            pl.BlockSpec((d, nh * LANES), const),
            pl.BlockSpec((d, nh * LANES), const),
            pl.BlockSpec((d, nrest), const),
            pl.BlockSpec((1, nh * LANES), const),
            pl.BlockSpec((1, nh * LANES), const),
            pl.BlockSpec((1, nh * LANES), const),
            pl.BlockSpec((1, SWA_HEADS * dh), const),
            pl.BlockSpec((1, SWA_KV_HEADS * dh), const),
            pl.BlockSpec((CUM_CHUNK, 3 * CUM_CHUNK), const),
        ],
        out_specs=[
            pl.BlockSpec((tm, nh * LANES), row),
            pl.BlockSpec((tm, nh * LANES), row),
            pl.BlockSpec((tm, nh * dh), row),
            pl.BlockSpec((tm, SWA_HEADS * dh), row),
            pl.BlockSpec((tm, SWA_KV_HEADS * dh), row),
            pl.BlockSpec((tm, SWA_KV_HEADS * dh), row),
        ],
        out_shape=out_shapes,
        scratch_shapes=[pltpu.VMEM((8, nh * LANES), F32)],
        compiler_params=pltpu.CompilerParams(dimension_semantics=("arbitrary",), vmem_limit_bytes=VMEM_LIMIT_BYTES),
        name="proj",
    )(x2d, norm_mix.reshape(1, d), wqa, wka, wrest, bf_row, gqa, gka, gsq, gsk, tri3)


def _fox_kernel(qa_ref, ka_ref, v_ref, o_ref, *, blk):
    i = pl.program_id(2)
    nt = (((1,), (1,)), ((), ()))
    row = lax.broadcasted_iota(jnp.int32, (blk, blk), 0)
    col = lax.broadcasted_iota(jnp.int32, (blk, blk), 1)
    causal = col <= row
    outs = []
    for hh in range(2):
        q = qa_ref[:, hh * LANES:(hh + 1) * LANES]

        def scores(j):
            start = pl.multiple_of(j * blk, blk)
            k = ka_ref[pl.ds(start, blk), hh * LANES:(hh + 1) * LANES]
            v = v_ref[pl.ds(start, blk), :]
            return lax.dot_general(q, k, nt, preferred_element_type=F32), v

        s, v = scores(i)
        s = jnp.where(causal, s, NEG_INF)
        m = jnp.max(s, axis=-1, keepdims=True)
        p = jnp.exp(s - m)
        l = jnp.sum(p, axis=-1, keepdims=True)
        acc = jnp.dot(p.astype(BF16), v, preferred_element_type=F32)

        def body(j, carry):
            m, l, acc = carry
            s, v = scores(j)
            m_new = jnp.maximum(m, jnp.max(s, axis=-1, keepdims=True))
            alpha = jnp.exp(m - m_new)
            p = jnp.exp(s - m_new)
            l = alpha * l + jnp.sum(p, axis=-1, keepdims=True)
            acc = alpha * acc + jnp.dot(p.astype(BF16), v, preferred_element_type=F32)
            return m_new, l, acc

        m, l, acc = lax.fori_loop(0, i, body, (m, l, acc))
        outs.append(acc / l)
    lane = _lane_iota(outs[0].shape)
    o_ref[...] = jnp.where(lane < HEAD_DIM, outs[0], outs[1]).astype(o_ref.dtype)


def _fox(qa, ka, fv, batch, seq):
    nh, dh = FOX_HEADS, HEAD_DIM
    blk = min(512, seq)
    qa3 = qa.reshape(batch, seq, nh * LANES)
    ka3 = ka.reshape(batch, seq, nh * LANES)
    fv3 = fv.reshape(batch, seq, nh * dh)
    out = pl.pallas_call(
        functools.partial(_fox_kernel, blk=blk),
        grid=(batch, nh // 2, seq // blk),
        in_specs=[
            pl.BlockSpec((None, blk, 2 * LANES), lambda b, j, i: (b, i, j)),
            pl.BlockSpec((None, seq, 2 * LANES), lambda b, j, i: (b, 0, j)),
            pl.BlockSpec((None, seq, LANES), lambda b, j, i: (b, 0, j)),
        ],
        out_specs=pl.BlockSpec((None, blk, LANES), lambda b, j, i: (b, i, j)),
        out_shape=jax.ShapeDtypeStruct((batch, seq, nh * dh), BF16),
        compiler_params=pltpu.CompilerParams(dimension_semantics=("arbitrary", "arbitrary", "arbitrary"),
                                             vmem_limit_bytes=VMEM_LIMIT_BYTES),
        name="fox",
    )(qa3, ka3, fv3)
    return out.reshape(batch * seq, nh * dh)


def _t5_bucket_table():
    qi = np.arange(WINDOW)[:, None]
    kj = np.arange(2 * WINDOW)[None, :]
    dist = qi + WINDOW - kj
    n = np.maximum(dist, 0)
    max_exact = N_BUCKETS // 2
    nf = np.maximum(n, 1).astype(np.float64)
    large = max_exact + (np.log(nf / max_exact) / math.log(MAX_DISTANCE / max_exact) * (N_BUCKETS - max_exact)).astype(np.int64)
    large = np.minimum(large, N_BUCKETS - 1)
    bucket = np.where(n < max_exact, n, large)
    valid = (dist >= 0) & (dist < WINDOW)
    return np.where(valid, bucket, -1).astype(np.int32)


def _swa_kernel(rb_ref, sinks_ref, bucket_ref, sq_ref, sk_ref, sv_ref, skp_ref, svp_ref, o_ref, bias_ref, *, blocks_per_tile):
    b = pl.program_id(0)
    i = pl.program_id(1)
    w = WINDOW
    g = SWA_GROUP

    @pl.when((b == 0) & (i == 0))
    def _():
        bucket = bucket_ref[...]
        for h in range(SWA_HEADS):
            acc = jnp.full((w, 2 * w), NEG_INF, F32)
            for bk in range(N_BUCKETS):
                acc = jnp.where(bucket == bk, rb_ref[bk, h], acc)
            kv, gi = divmod(h, g)
            bias_ref[kv, gi * w:(gi + 1) * w, :] = acc

    nt = (((1,), (1,)), ((), ()))
    lane = _lane_iota((w, LANES))
    col = _lane_iota((g * w, 2 * w))
    for cb in range(blocks_per_tile):
        rows = slice(cb * w, (cb + 1) * w)
        if cb == 0:
            kprev, vprev = skp_ref[...], svp_ref[...]
        else:
            kprev, vprev = sk_ref[(cb - 1) * w:cb * w, :], sv_ref[(cb - 1) * w:cb * w, :]
        kcat = jnp.concatenate([kprev, sk_ref[rows, :]], axis=0)
        vcat = jnp.concatenate([vprev, sv_ref[rows, :]], axis=0)
        outs = []
        for kv in range(SWA_KV_HEADS):
            keep = (lane < HEAD_DIM) if kv == 0 else (lane >= HEAD_DIM)
            qs = [jnp.where(keep, sq_ref[rows, gi * LANES:(gi + 1) * LANES], jnp.zeros((), BF16)) for gi in range(g)]
            qstack = jnp.concatenate(qs, axis=0)
            s = lax.dot_general(qstack, kcat, nt, preferred_element_type=F32) + bias_ref[kv]
            if cb == 0:
                s = s + jnp.where(col < w, jnp.where(i == 0, NEG_INF, 0.0), 0.0)
            sink = jnp.concatenate([jnp.full((w, 1), sinks_ref[kv * g + gi], F32) for gi in range(g)], axis=0)
            m = jnp.maximum(jnp.max(s, axis=-1, keepdims=True), sink)
            p = jnp.exp(s - m)
            denom = jnp.sum(p, axis=-1, keepdims=True) + jnp.exp(sink - m)
            outs.append(jnp.dot(p.astype(BF16), vcat, preferred_element_type=F32) / denom)
        for gi in range(g):
            o_ref[rows, gi * LANES:(gi + 1) * LANES] = jnp.where(
                lane < HEAD_DIM, outs[0][gi * w:(gi + 1) * w, :], outs[1][gi * w:(gi + 1) * w, :]).astype(o_ref.dtype)


def _swa(sq, sk, sv, sinks, rel_bias, batch, seq):
    w = WINDOW
    dh = HEAD_DIM
    blocks_per_tile = min(4, seq // w)
    t = blocks_per_tile * w
    assert seq % t == 0
    sq3 = sq.reshape(batch, seq, SWA_HEADS * dh)
    sk3 = sk.reshape(batch, seq, SWA_KV_HEADS * dh)
    sv3 = sv.reshape(batch, seq, SWA_KV_HEADS * dh)
    bucket = jnp.asarray(_t5_bucket_table())
    cur = lambda b, i: (b, i, 0)
    prev = lambda b, i: (b, jnp.maximum(i * blocks_per_tile - 1, 0), 0)
    smem = pl.BlockSpec(memory_space=pltpu.SMEM)
    out = pl.pallas_call(
        functools.partial(_swa_kernel, blocks_per_tile=blocks_per_tile),
        grid=(batch, seq // t),
        in_specs=[
            smem, smem,
            pl.BlockSpec((w, 2 * w), lambda b, i: (0, 0)),
            pl.BlockSpec((None, t, SWA_HEADS * dh), cur),
            pl.BlockSpec((None, t, SWA_KV_HEADS * dh), cur),
            pl.BlockSpec((None, t, SWA_KV_HEADS * dh), cur),
            pl.BlockSpec((None, w, SWA_KV_HEADS * dh), prev),
            pl.BlockSpec((None, w, SWA_KV_HEADS * dh), prev),
        ],
        out_specs=pl.BlockSpec((None, t, SWA_HEADS * dh), cur),
        out_shape=jax.ShapeDtypeStruct((batch, seq, SWA_HEADS * dh), BF16),
        scratch_shapes=[pltpu.VMEM((SWA_KV_HEADS, SWA_GROUP * w, 2 * w), F32)],
        compiler_params=pltpu.CompilerParams(dimension_semantics=("arbitrary", "arbitrary"),
                                             vmem_limit_bytes=VMEM_LIMIT_BYTES),
        name="swa",
    )(rel_bias, sinks, bucket, sq3, sk3, sv3, sk3, sv3)
    return out.reshape(batch * seq, SWA_HEADS * dh)


def _outproj_kernel(x_ref, fa_ref, sb_ref, wa_ref, wb_ref, g_ref, wrh_ref, wrl_ref, br_ref, lstrict_ref,
                    h_ref, xs_ref, meta_ref, gates_ref, counts_ref, carry_ref, *, n_experts):
    i = pl.program_id(0)
    tm = x_ref.shape[0]

    @pl.when(i == 0)
    def _():
        carry_ref[...] = jnp.zeros_like(carry_ref)

    h = (x_ref[...] + jnp.dot(fa_ref[...], wa_ref[...], preferred_element_type=F32)
         + jnp.dot(sb_ref[...], wb_ref[...], preferred_element_type=F32))
    h_ref[...] = h
    ms = jnp.mean(h * h, axis=-1, keepdims=True)
    hn = h * lax.rsqrt(ms + EPS) * g_ref[...]
    xs_ref[...] = _pack_bf16_pairs(hn)

    hn_hi = hn.astype(BF16)
    hn_lo = (hn - hn_hi.astype(F32)).astype(BF16)
    wrh = wrh_ref[...]
    logits = (jnp.dot(hn_hi, wrh, preferred_element_type=F32) + jnp.dot(hn_lo, wrh, preferred_element_type=F32)
              + jnp.dot(hn_hi, wrl_ref[...], preferred_element_type=F32) + br_ref[...])
    lane = _lane_iota((tm, LANES))
    lane_f = lane.astype(F32)
    l = jnp.where(lane < n_experts, logits, NEG_INF)
    vals, idxs = [], []
    member = jnp.zeros((tm, LANES), F32)
    for _ in range(TOP_K):
        mk = jnp.max(l, axis=-1, keepdims=True)
        ik = jnp.min(jnp.where(l == mk, lane_f, float(LANES)), axis=-1, keepdims=True)
        sel = lane_f == ik
        l = jnp.where(sel, NEG_INF, l)
        member = jnp.where(sel, 1.0, member)
        vals.append(mk)
        idxs.append(ik)
    es = [jnp.exp(v - vals[0]) for v in vals]
    tot = es[0] + es[1] + es[2] + es[3]
    gates = [e / tot for e in es]

    rank = jnp.dot(lstrict_ref[...], member.astype(BF16), preferred_element_type=F32) + carry_ref[0:1, :]
    carry_new = rank[tm - 1:tm, :] + member[tm - 1:tm, :]
    carry_ref[0:1, :] = carry_new
    counts_ref[...] = jnp.broadcast_to(carry_new, counts_ref.shape)
    ranks = [jnp.sum(jnp.where(lane_f == ik, rank, 0.0), axis=-1, keepdims=True) for ik in idxs]

    meta = jnp.zeros((tm, LANES), F32)
    gt = jnp.zeros((tm, LANES), F32)
    for k in range(TOP_K):
        meta = jnp.where(lane == k, idxs[k], meta)
        meta = jnp.where(lane == TOP_K + k, ranks[k], meta)
        gt = jnp.where(lane == k, gates[k], gt)
    meta_ref[...] = meta.astype(jnp.int32)
    gates_ref[...] = gt


def _outproj(x2d, fa, sb, w_out, norm_moe, w_router, b_router):
    n, d = x2d.shape
    n_experts = w_router.shape[1]
    assert n_experts <= LANES
    tm = min(512, n)
    assert n % tm == 0
    nfa = FOX_HEADS * HEAD_DIM
    wa = w_out[:nfa].astype(BF16)
    wb = w_out[nfa:].reshape(SWA_KV_HEADS, SWA_GROUP, HEAD_DIM, d).transpose(1, 0, 2, 3).reshape(SWA_HEADS * HEAD_DIM, d)
    wb = wb.astype(BF16)
    wr = jnp.zeros((d, LANES), F32).at[:, :n_experts].set(w_router)
    wr_hi = wr.astype(BF16)
    wr_lo = (wr - wr_hi.astype(F32)).astype(BF16)
    br = jnp.zeros((1, LANES), F32).at[0, :n_experts].set(b_router)
    lstrict = jnp.asarray(np.tril(np.ones((tm, tm), np.float32), -1), BF16)
    const = lambda i: (0, 0)
    row = lambda i: (i, 0)
    return pl.pallas_call(
        functools.partial(_outproj_kernel, n_experts=n_experts),
        grid=(n // tm,),
        in_specs=[
            pl.BlockSpec((tm, d), row),
            pl.BlockSpec((tm, nfa), row),
            pl.BlockSpec((tm, SWA_HEADS * HEAD_DIM), row),
            pl.BlockSpec((nfa, d), const),
            pl.BlockSpec((SWA_HEADS * HEAD_DIM, d), const),
            pl.BlockSpec((1, d), const),
            pl.BlockSpec((d, LANES), const),
            pl.BlockSpec((d, LANES), const),
            pl.BlockSpec((1, LANES), const),
            pl.BlockSpec((tm, tm), const),
        ],
        out_specs=[
            pl.BlockSpec((tm, d), row),
            pl.BlockSpec((tm, d // 2), row),
            pl.BlockSpec((tm, LANES), row),
            pl.BlockSpec((tm, LANES), row),
            pl.BlockSpec((8, LANES), const),
        ],
        out_shape=(
            jax.ShapeDtypeStruct((n, d), F32),
            jax.ShapeDtypeStruct((n, d // 2), jnp.uint32),
            jax.ShapeDtypeStruct((n, LANES), jnp.int32),
            jax.ShapeDtypeStruct((n, LANES), F32),
            jax.ShapeDtypeStruct((8, LANES), F32),
        ),
        scratch_shapes=[pltpu.VMEM((8, LANES), F32)],
        compiler_params=pltpu.CompilerParams(dimension_semantics=("arbitrary",), vmem_limit_bytes=VMEM_LIMIT_BYTES),
        name="outproj",
    )(x2d, fa, sb, wa, wb, norm_moe.reshape(1, d), wr_hi, wr_lo, br, lstrict)


def _dispatch_kernel(pos_ref, x_ref, init_ref, out_ref, sem):
    del init_ref
    tm = x_ref.shape[0]

    def row_copy(t, p):
        return pltpu.make_async_copy(x_ref.at[pl.ds(t, 1), :], out_ref.at[pl.ds(p, 1), :], sem)

    def start(t, c):
        for k in range(TOP_K):
            row_copy(t, pos_ref[0, 0, t * TOP_K + k]).start()
        return c

    def wait(t, c):
        for k in range(TOP_K):
            row_copy(t, pos_ref[0, 0, t * TOP_K + k]).wait()
        return c

    lax.fori_loop(0, tm, start, 0)
    lax.fori_loop(0, tm, wait, 0)


def _dispatch(xs_tok, pos, rows_total):
    n, w = xs_tok.shape
    tm = min(512, n)
    pos3 = pos.reshape(n // tm, 1, tm * TOP_K)
    init = jnp.zeros((rows_total, w), xs_tok.dtype)
    return pl.pallas_call(
        _dispatch_kernel,
        grid=(n // tm,),
        in_specs=[
            pl.BlockSpec((1, 1, tm * TOP_K), lambda i: (i, 0, 0), memory_space=pltpu.SMEM),
            pl.BlockSpec((tm, w), lambda i: (i, 0)),
            pl.BlockSpec(memory_space=pl.ANY),
        ],
        out_specs=pl.BlockSpec(memory_space=pl.ANY),
        out_shape=jax.ShapeDtypeStruct((rows_total, w), xs_tok.dtype),
        scratch_shapes=[pltpu.SemaphoreType.DMA(())],
        input_output_aliases={2: 0},
        compiler_params=pltpu.CompilerParams(dimension_semantics=("arbitrary",), vmem_limit_bytes=VMEM_LIMIT_BYTES,
                                             has_side_effects=True),
        name="dispatch",
    )(pos3, xs_tok, init)


def _experts_kernel(be_ref, bv_ref, x_ref, wug_ref, bug_ref, wd_ref, bd_ref, y_ref, wug_bf, wd_bf):
    i = pl.program_id(0)
    f = wd_ref.shape[0]
    prev = be_ref[jnp.maximum(i - 1, 0)]

    @pl.when((i == 0) | (be_ref[i] != prev))
    def _():
        wug_bf[...] = wug_ref[...].astype(BF16)
        wd_bf[...] = wd_ref[...].astype(BF16)

    @pl.when(bv_ref[i] > 0)
    def _():
        first, second = _unpack_bf16_pairs(x_ref[...])
        x = jnp.concatenate([first.astype(BF16), second.astype(BF16)], axis=1)
        gu = jnp.dot(x, wug_bf[...], preferred_element_type=F32) + bug_ref[...]
        gate = jnp.minimum(gu[:, :f], SWIGLU_LIMIT)
        up = jnp.clip(gu[:, f:], -SWIGLU_LIMIT, SWIGLU_LIMIT)
        act = (up + 1.0) * (gate * (1.0 / (1.0 + jnp.exp(-SWIGLU_ALPHA * gate))))
        y = jnp.dot(act.astype(BF16), wd_bf[...], preferred_element_type=F32) + bd_ref[...]
        y_ref[...] = _pack_bf16_pairs(y)

    @pl.when(bv_ref[i] <= 0)
    def _():
        y_ref[...] = jnp.zeros_like(y_ref)


def _experts(xs_sorted, blk_exp, blk_valid, w_up_gate, b_up_gate, w_down, b_down, tr):
    rows_total, w = xs_sorted.shape
    e, d, f2 = w_up_gate.shape
    f = f2 // 2
    grid_spec = pltpu.PrefetchScalarGridSpec(
        num_scalar_prefetch=2,
        grid=(rows_total // tr,),
        in_specs=[
            pl.BlockSpec((tr, w), lambda i, be, bv: (i, 0)),
            pl.BlockSpec((None, d, f2), lambda i, be, bv: (be[i], 0, 0)),
            pl.BlockSpec((None, 1, f2), lambda i, be, bv: (be[i], 0, 0)),
            pl.BlockSpec((None, f, d), lambda i, be, bv: (be[i], 0, 0)),
            pl.BlockSpec((None, 1, d), lambda i, be, bv: (be[i], 0, 0)),
        ],
        out_specs=pl.BlockSpec((tr, d // 2), lambda i, be, bv: (i, 0)),
        scratch_shapes=[pltpu.VMEM((d, f2), BF16), pltpu.VMEM((f, d), BF16)],
    )
    return pl.pallas_call(
        _experts_kernel,
        grid_spec=grid_spec,
        out_shape=jax.ShapeDtypeStruct((rows_total, d // 2), jnp.uint32),
        compiler_params=pltpu.CompilerParams(dimension_semantics=("arbitrary",), vmem_limit_bytes=VMEM_LIMIT_BYTES),
        name="experts",
    )(blk_exp, blk_valid, xs_sorted, w_up_gate, b_up_gate.reshape(e, 1, f2), w_down, b_down.reshape(e, 1, d))


def _combine_kernel(pos_ref, posn_ref, h_ref, gates_ref, ys_ref, o_ref, buf, sems):
    i = pl.program_id(0)
    n_steps = pl.num_programs(0)
    tm = h_ref.shape[0]
    slot = i % 2

    def row_copy(p_ref, t, k, s):
        p = p_ref[0, 0, t * TOP_K + k]
        return pltpu.make_async_copy(ys_ref.at[pl.ds(p, 1), :], buf.at[s, k, pl.ds(t, 1), :], sems.at[s, k])

    def issue(p_ref, s):
        def body(t, c):
            for k in range(TOP_K):
                row_copy(p_ref, t, k, s).start()
            return c
        lax.fori_loop(0, tm, body, 0)

    @pl.when(i == 0)
    def _():
        issue(pos_ref, 0)

    @pl.when(i + 1 < n_steps)
    def _():
        issue(posn_ref, 1 - slot)

    def wait_body(t, c):
        for k in range(TOP_K):
            row_copy(pos_ref, t, k, slot).wait()
        return c
    lax.fori_loop(0, tm, wait_body, 0)

    acc = h_ref[...]
    gates = gates_ref[...]
    for k in range(TOP_K):
        first, second = _unpack_bf16_pairs(buf[slot, k])
        acc = acc + gates[:, k:k + 1] * jnp.concatenate([first, second], axis=1)
    o_ref[...] = acc


def _combine(h, gates, pos, ys):
    n, d = h.shape
    tm = min(256, n)
    nt = n // tm
    pos3 = pos.reshape(nt, 1, tm * TOP_K)
    smem_blk = lambda imap: pl.BlockSpec((1, 1, tm * TOP_K), imap, memory_space=pltpu.SMEM)
    return pl.pallas_call(
        _combine_kernel,
        grid=(nt,),
        in_specs=[
            smem_blk(lambda i: (i, 0, 0)),
            smem_blk(lambda i: (jnp.minimum(i + 1, nt - 1), 0, 0)),
            pl.BlockSpec((tm, d), lambda i: (i, 0)),
            pl.BlockSpec((tm, LANES), lambda i: (i, 0)),
            pl.BlockSpec(memory_space=pl.ANY),
        ],
        out_specs=pl.BlockSpec((tm, d), lambda i: (i, 0)),
        out_shape=jax.ShapeDtypeStruct((n, d), F32),
        scratch_shapes=[pltpu.VMEM((2, TOP_K, tm, d // 2), jnp.uint32), pltpu.SemaphoreType.DMA((2, TOP_K))],
        compiler_params=pltpu.CompilerParams(dimension_semantics=("arbitrary",), vmem_limit_bytes=VMEM_LIMIT_BYTES),
        name="combine",
    )(pos3, pos3, h, gates, ys)


EXPERT_TILE = 256


def _layer(h2d, batch, seq, norm_mix, w_in, b_forget, q_norm_fox, k_norm_fox, q_norm_swa, k_norm_swa, sinks, rel_bias,
           w_out, norm_moe, w_router, b_router, w_up_gate, b_up_gate, w_down, b_down):
    n, d = h2d.shape
    n_experts = w_router.shape[1]
    qa, ka, fv, sq, sk, sv = _proj(h2d, norm_mix, w_in, b_forget, q_norm_fox, k_norm_fox, q_norm_swa, k_norm_swa, seq)
    fa = _fox(qa, ka, fv, batch, seq)
    sb = _swa(sq, sk, sv, sinks, rel_bias, batch, seq)
    h, xs_tok, meta, gates, counts = _outproj(h2d, fa, sb, w_out, norm_moe, w_router, b_router)

    tr = min(EXPERT_TILE, n)
    counts_i = counts[0, :n_experts].astype(jnp.int32)
    padded = (counts_i + tr - 1) // tr * tr
    pend = jnp.cumsum(padded)
    pstart = pend - padded
    idx = meta[:, :TOP_K]
    rank = meta[:, TOP_K:2 * TOP_K]
    pos = pstart[idx] + rank
    rows_total = n * TOP_K + n_experts * tr
    n_tiles = rows_total // tr
    blk_start = jnp.arange(n_tiles, dtype=jnp.int32) * tr
    blk_exp = jnp.minimum(jnp.sum(pend[None, :] <= blk_start[:, None], axis=1), n_experts - 1).astype(jnp.int32)
    blk_valid = jnp.clip(pstart[blk_exp] + counts_i[blk_exp] - blk_start, 0, tr).astype(jnp.int32)

    xs_sorted = _dispatch(xs_tok, pos.astype(jnp.int32), rows_total)
    ys = _experts(xs_sorted, blk_exp, blk_valid, w_up_gate, b_up_gate, w_down, b_down, tr)
    return _combine(h, gates, pos.astype(jnp.int32), ys)


def kernel(x, norm_mix, w_in, b_forget, q_norm_fox, k_norm_fox, q_norm_swa, k_norm_swa, sinks, rel_bias, w_out, norm_moe,
           w_router, b_router, w_up_gate, b_up_gate, w_down, b_down):
    batch, seq, d = x.shape
    h = x.reshape(batch * seq, d)
    for l in range(norm_mix.shape[0]):
        h = _layer(h, batch, seq, norm_mix[l], w_in[l], b_forget[l], q_norm_fox[l], k_norm_fox[l], q_norm_swa[l],
                   k_norm_swa[l], sinks[l], rel_bias, w_out[l], norm_moe[l], w_router[l], b_router[l], w_up_gate[l],
                   b_up_gate[l], w_down[l], b_down[l])
    return h.reshape(batch, seq, d)
```

```python
import functools
import math

import numpy as np
import jax
import jax.numpy as jnp
from jax import lax
from jax.experimental import pallas as pl
from jax.experimental.pallas import tpu as pltpu

HEAD_DIM = 64
FOX_HEADS = 8
SWA_HEADS = 8
SWA_KV_HEADS = 2
SWA_GROUP = SWA_HEADS // SWA_KV_HEADS
WINDOW = 128
N_BUCKETS = 32
MAX_DISTANCE = 128
TOP_K = 4
SWIGLU_LIMIT = 7.0
SWIGLU_ALPHA = 1.702
EPS = 1e-6

LANES = 128
VMEM_LIMIT_BYTES = 56 * 1024 * 1024
NEG_INF = float("-inf")
LOG2E = math.log2(math.e)

F32 = jnp.float32
BF16 = jnp.bfloat16


def _lane_iota(shape):
    return lax.broadcasted_iota(jnp.int32, shape, len(shape) - 1)


def _split3_bf16(v):
    hi = v.astype(BF16)
    r1 = v - hi.astype(F32)
    mid = r1.astype(BF16)
    lo = (r1 - mid.astype(F32)).astype(BF16)
    return hi, mid, lo


def _pack_bf16_pairs(v):
    w = v.shape[1] // 2
    u = pltpu.bitcast(v.astype(BF16).astype(F32), jnp.uint32)
    return u[:, :w] | (u[:, w:] >> 16)


def _unpack_bf16_pairs(u):
    first = pltpu.bitcast(u & jnp.uint32(0xFFFF0000), F32)
    second = pltpu.bitcast(u << 16, F32)
    return first, second


CUM_CHUNK = 128


def _proj_kernel(x_ref, g_ref, wqa_ref, wka_ref, wrest_ref, bf_ref, gqa_ref, gka_ref, gsq_ref, gsk_ref, tri_ref,
                 qa_ref, ka_ref, fv_ref, sq_ref, sk_ref, sv_ref, carry_ref, *, tiles_per_seq):
    i = pl.program_id(0)
    tm = x_ref.shape[0]
    nh = FOX_HEADS
    scale = HEAD_DIM ** -0.5

    @pl.when(i % tiles_per_seq == 0)
    def _():
        carry_ref[...] = jnp.zeros_like(carry_ref)

    x = x_ref[...]
    ms = jnp.mean(x * x, axis=-1, keepdims=True)
    hn = (x * lax.rsqrt(ms + EPS) * g_ref[...]).astype(BF16)

    qa = jnp.dot(hn, wqa_ref[...], preferred_element_type=F32)
    ka = jnp.dot(hn, wka_ref[...], preferred_element_type=F32)
    rest = jnp.dot(hn, wrest_ref[...], preferred_element_type=F32)

    lane_w = _lane_iota((tm, nh * LANES)) % LANES
    upper = lane_w >= HEAD_DIM

    fl = qa + bf_ref[...]
    ls = jnp.minimum(fl, 0.0) - jnp.log(1.0 + jnp.exp(-jnp.abs(fl)))
    ls = jnp.where(upper, ls, 0.0)
    tri = tri_ref[...]
    carry = carry_ref[0:1, :]
    c_chunks = []
    for r in range(tm // CUM_CHUNK):
        hi, mid, lo = _split3_bf16(ls[r * CUM_CHUNK:(r + 1) * CUM_CHUNK, :])
        cs = jnp.dot(tri, jnp.concatenate([hi, mid, lo], axis=0), preferred_element_type=F32) + carry
        carry = cs[CUM_CHUNK - 1:CUM_CHUNK, :]
        c_chunks.append(cs)
    carry_ref[0:1, :] = carry
    c = jnp.concatenate(c_chunks, axis=0) * LOG2E

    c1 = c.astype(BF16).astype(F32)
    c2 = (c - c1).astype(BF16).astype(F32)
    c3 = c - c1 - c2

    def head_norm(t, g_row):
        outs = []
        for h in range(t.shape[1] // LANES):
            th = t[:, h * LANES:(h + 1) * LANES]
            lo_mask = _lane_iota(th.shape) < HEAD_DIM
            ss = jnp.sum(jnp.where(lo_mask, th * th, 0.0), axis=-1, keepdims=True)
            outs.append(th * lax.rsqrt(ss * (1.0 / HEAD_DIM) + EPS))
        return jnp.concatenate(outs, axis=1) * g_row

    qn = head_norm(qa, gqa_ref[...]) * (scale * LOG2E)
    kn = head_norm(ka, gka_ref[...])

    one = jnp.float32(1.0)
    zero = jnp.float32(0.0)
    q_aug = jnp.where(lane_w == 64, c1, jnp.where(lane_w == 65, c2, jnp.where(lane_w == 66, c3,
                      jnp.where(lane_w < 70, one, zero))))
    k_aug = jnp.where(lane_w < 67, one, jnp.where(lane_w == 67, -c1, jnp.where(lane_w == 68, -c2,
                      jnp.where(lane_w == 69, -c3, zero))))
    qa_ref[...] = jnp.where(upper, q_aug, qn).astype(BF16)
    ka_ref[...] = jnp.where(upper, k_aug, kn).astype(BF16)

    nfv = FOX_HEADS * HEAD_DIM
    nsq = SWA_HEADS * HEAD_DIM
    nsk = SWA_KV_HEADS * HEAD_DIM
    fv_ref[...] = rest[:, :nfv].astype(BF16)

    def half_norm(t, g_row):
        outs = []
        for j in range(t.shape[1] // LANES):
            tj = t[:, j * LANES:(j + 1) * LANES]
            lo_mask = _lane_iota(tj.shape) < HEAD_DIM
            sq = tj * tj
            s_lo = jnp.sum(jnp.where(lo_mask, sq, 0.0), axis=-1, keepdims=True)
            s_hi = jnp.sum(jnp.where(lo_mask, 0.0, sq), axis=-1, keepdims=True)
            r = jnp.where(lo_mask, lax.rsqrt(s_lo * (1.0 / HEAD_DIM) + EPS), lax.rsqrt(s_hi * (1.0 / HEAD_DIM) + EPS))
            outs.append(tj * r)
        return jnp.concatenate(outs, axis=1) * g_row

    sq_ref[...] = (half_norm(rest[:, nfv:nfv + nsq], gsq_ref[...]) * scale).astype(BF16)
    sk_ref[...] = half_norm(rest[:, nfv + nsq:nfv + nsq + nsk], gsk_ref[...]).astype(BF16)
    sv_ref[...] = rest[:, nfv + nsq + nsk:].astype(BF16)


def _proj(x2d, norm_mix, w_in, b_forget, q_norm_fox, k_norm_fox, q_norm_swa, k_norm_swa, seq):
    n, d = x2d.shape
    nh, dh = FOX_HEADS, HEAD_DIM
    tm = min(512, seq)
    assert seq % tm == 0 and tm % CUM_CHUNK == 0
    points = np.cumsum([nh * dh, nh * dh, nh * dh, nh, SWA_HEADS * dh, SWA_KV_HEADS * dh, SWA_KV_HEADS * dh])
    wfq, wfk, wfv, wff, wsq, wsk, wsv = jnp.split(w_in, [int(p) for p in points[:-1]], axis=1)

    wfq = wfq.reshape(d, nh, dh)
    wff_rep = jnp.broadcast_to(wff[:, :, None], (d, nh, dh))
    wqa = jnp.concatenate([wfq, wff_rep], axis=2).reshape(d, nh * LANES).astype(BF16)
    wka = jnp.concatenate([wfk.reshape(d, nh, dh), jnp.zeros((d, nh, dh), w_in.dtype)], axis=2)
    wka = wka.reshape(d, nh * LANES).astype(BF16)
    wsq = wsq.reshape(d, SWA_KV_HEADS, SWA_GROUP, dh).transpose(0, 2, 1, 3).reshape(d, SWA_HEADS * dh)
    wrest = jnp.concatenate([wfv, wsq, wsk, wsv], axis=1).astype(BF16)

    zeros_h = jnp.zeros((nh, dh), F32)
    bf_row = jnp.concatenate([zeros_h, jnp.broadcast_to(b_forget[:, None], (nh, dh))], axis=1).reshape(1, nh * LANES)
    gqa = jnp.concatenate([jnp.broadcast_to(q_norm_fox[None], (nh, dh)), zeros_h], axis=1).reshape(1, nh * LANES)
    gka = jnp.concatenate([jnp.broadcast_to(k_norm_fox[None], (nh, dh)), zeros_h], axis=1).reshape(1, nh * LANES)
    gsq = jnp.tile(q_norm_swa, SWA_HEADS).reshape(1, SWA_HEADS * dh)
    gsk = jnp.tile(k_norm_swa, SWA_KV_HEADS).reshape(1, SWA_KV_HEADS * dh)
    tri = np.tril(np.ones((CUM_CHUNK, CUM_CHUNK), np.float32))
    tri3 = jnp.asarray(np.concatenate([tri, tri, tri], axis=1), BF16)

    nrest = wrest.shape[1]
    const = lambda i: (0, 0)
    row = lambda i: (i, 0)
    out_shapes = (
        jax.ShapeDtypeStruct((n, nh * LANES), BF16),
        jax.ShapeDtypeStruct((n, nh * LANES), BF16),
        jax.ShapeDtypeStruct((n, nh * dh), BF16),
        jax.ShapeDtypeStruct((n, SWA_HEADS * dh), BF16),
        jax.ShapeDtypeStruct((n, SWA_KV_HEADS * dh), BF16),
        jax.ShapeDtypeStruct((n, SWA_KV_HEADS * dh), BF16),
    )
    return pl.pallas_call(
        functools.partial(_proj_kernel, tiles_per_seq=seq // tm),
        grid=(n // tm,),
        in_specs=[
            pl.BlockSpec((tm, d), row),
            pl.BlockSpec((1, d), const),
            pl.BlockSpec((d, nh * LANES), const),
            pl.BlockSpec((d, nh * LANES), const),
            pl.BlockSpec((d, nrest), const),
            pl.BlockSpec((1, nh * LANES), const),
            pl.BlockSpec((1, nh * LANES), const),
            pl.BlockSpec((1, nh * LANES), const),
            pl.BlockSpec((1, SWA_HEADS * dh), const),
            pl.BlockSpec((1, SWA_KV_HEADS * dh), const),
            pl.BlockSpec((CUM_CHUNK, 3 * CUM_CHUNK), const),
        ],
        out_specs=[
            pl.BlockSpec((tm, nh * LANES), row),
            pl.BlockSpec((tm, nh * LANES), row),
            pl.BlockSpec((tm, nh * dh), row),
            pl.BlockSpec((tm, SWA_HEADS * dh), row),
            pl.BlockSpec((tm, SWA_KV_HEADS * dh), row),
            pl.BlockSpec((tm, SWA_KV_HEADS * dh), row),
        ],
        out_shape=out_shapes,
        scratch_shapes=[pltpu.VMEM((8, nh * LANES), F32)],
        compiler_params=pltpu.CompilerParams(dimension_semantics=("arbitrary",), vmem_limit_bytes=VMEM_LIMIT_BYTES),
        name="proj",
    )(x2d, norm_mix.reshape(1, d), wqa, wka, wrest, bf_row, gqa, gka, gsq, gsk, tri3)


FOX_BLOCK = 512


def _fox_kernel(qa_ref, ka_ref, v_ref, o_ref, s0_sc, s1_sc, m_sc, l_sc, acc_sc, *, blk):
    i = pl.program_id(2)
    nt = (((1,), (1,)), ((), ()))
    row = lax.broadcasted_iota(jnp.int32, (blk, blk), 0)
    col = lax.broadcasted_iota(jnp.int32, (blk, blk), 1)
    causal = col <= row
    s_bufs = (s0_sc, s1_sc)

    def lane_tiles(s):
        return [s[:, c * LANES:(c + 1) * LANES] for c in range(s.shape[1] // LANES)]

    def row_max(tiles):
        m = jnp.max(functools.reduce(jnp.maximum, tiles), axis=-1, keepdims=True)
        return jnp.broadcast_to(m, (blk, LANES))

    def row_sum(tiles):
        t = jnp.sum(functools.reduce(jnp.add, tiles), axis=-1, keepdims=True)
        return jnp.broadcast_to(t, (blk, LANES))

    def produce_scores(j, slot):
        start = pl.multiple_of(j * blk, blk)
        for hh in range(2):
            q = qa_ref[:, hh * LANES:(hh + 1) * LANES]
            k = ka_ref[pl.ds(start, blk), hh * LANES:(hh + 1) * LANES]
            s_bufs[slot][hh] = lax.dot_general(q, k, nt, preferred_element_type=F32)

    def consume_scores(j, slot, masked):
        v = v_ref[pl.ds(pl.multiple_of(j * blk, blk), blk), :]
        for hh in range(2):
            s = s_bufs[slot][hh]
            if masked:
                s = jnp.where(causal, s, NEG_INF)
            tiles = lane_tiles(s)
            m = m_sc[hh]
            m_new = jnp.maximum(m, row_max(tiles))
            alpha = jnp.exp2(m - m_new)
            p_tiles = [jnp.exp2(t - m_new) for t in tiles]
            p = jnp.concatenate(p_tiles, axis=1).astype(BF16)
            m_sc[hh] = m_new
            l_sc[hh] = alpha * l_sc[hh] + row_sum(p_tiles)
            acc_sc[hh] = alpha * acc_sc[hh] + jnp.dot(p, v, preferred_element_type=F32)

    m_sc[...] = jnp.full(m_sc.shape, NEG_INF, F32)
    l_sc[...] = jnp.zeros(l_sc.shape, F32)
    acc_sc[...] = jnp.zeros(acc_sc.shape, F32)
    produce_scores(0, 0)

    def pair(jj, c):
        j = 2 * jj
        produce_scores(j + 1, 1)
        consume_scores(j, 0, masked=False)
        produce_scores(j + 2, 0)
        consume_scores(j + 1, 1, masked=False)
        return c

    lax.fori_loop(0, i // 2, pair, 0)

    @pl.when(i % 2 == 0)
    def _():
        consume_scores(i, 0, masked=True)

    @pl.when(i % 2 == 1)
    def _():
        produce_scores(i, 1)
        consume_scores(i - 1, 0, masked=False)
        consume_scores(i, 1, masked=True)

    out0 = acc_sc[0] / l_sc[0]
    out1 = acc_sc[1] / l_sc[1]
    lane = _lane_iota(out0.shape)
    o_ref[...] = jnp.where(lane < HEAD_DIM, out0, out1).astype(o_ref.dtype)


def _fox(qa, ka, fv, batch, seq):
    nh, dh = FOX_HEADS, HEAD_DIM
    blk = min(FOX_BLOCK, seq)
    assert seq % blk == 0
    qa3 = qa.reshape(batch, seq, nh * LANES)
    ka3 = ka.reshape(batch, seq, nh * LANES)
    fv3 = fv.reshape(batch, seq, nh * dh)
    out = pl.pallas_call(
        functools.partial(_fox_kernel, blk=blk),
        grid=(batch, nh // 2, seq // blk),
        in_specs=[
            pl.BlockSpec((None, blk, 2 * LANES), lambda b, j, i: (b, i, j)),
            pl.BlockSpec((None, seq, 2 * LANES), lambda b, j, i: (b, 0, j)),
            pl.BlockSpec((None, seq, LANES), lambda b, j, i: (b, 0, j)),
        ],
        out_specs=pl.BlockSpec((None, blk, LANES), lambda b, j, i: (b, i, j)),
        out_shape=jax.ShapeDtypeStruct((batch, seq, nh * dh), BF16),
        scratch_shapes=[
            pltpu.VMEM((2, blk, blk), F32),
            pltpu.VMEM((2, blk, blk), F32),
            pltpu.VMEM((2, blk, LANES), F32),
            pltpu.VMEM((2, blk, LANES), F32),
            pltpu.VMEM((2, blk, LANES), F32),
        ],
        compiler_params=pltpu.CompilerParams(dimension_semantics=("arbitrary", "arbitrary", "arbitrary"),
                                             vmem_limit_bytes=VMEM_LIMIT_BYTES),
        name="fox",
    )(qa3, ka3, fv3)
    return out.reshape(batch * seq, nh * dh)


def _t5_bucket_table():
    qi = np.arange(WINDOW)[:, None]
    kj = np.arange(2 * WINDOW)[None, :]
    dist = qi + WINDOW - kj
    n = np.maximum(dist, 0)
    max_exact = N_BUCKETS // 2
    nf = np.maximum(n, 1).astype(np.float64)
    large = max_exact + (np.log(nf / max_exact) / math.log(MAX_DISTANCE / max_exact) * (N_BUCKETS - max_exact)).astype(np.int64)
    large = np.minimum(large, N_BUCKETS - 1)
    bucket = np.where(n < max_exact, n, large)
    valid = (dist >= 0) & (dist < WINDOW)
    return np.where(valid, bucket, -1).astype(np.int32)


def _swa_kernel(rb_ref, sinks_ref, bucket_ref, sq_ref, sk_ref, sv_ref, skp_ref, svp_ref, o_ref, bias_ref, *, blocks_per_tile):
    b = pl.program_id(0)
    i = pl.program_id(1)
    w = WINDOW
    g = SWA_GROUP

    @pl.when((b == 0) & (i == 0))
    def _():
        bucket = bucket_ref[...]
        for h in range(SWA_HEADS):
            acc = jnp.full((w, 2 * w), NEG_INF, F32)
            for bk in range(N_BUCKETS):
                acc = jnp.where(bucket == bk, rb_ref[bk, h], acc)
            kv, gi = divmod(h, g)
            bias_ref[kv, gi * w:(gi + 1) * w, :] = acc

    nt = (((1,), (1,)), ((), ()))
    lane = _lane_iota((w, LANES))
    col = _lane_iota((g * w, 2 * w))
    for cb in range(blocks_per_tile):
        rows = slice(cb * w, (cb + 1) * w)
        if cb == 0:
            kprev, vprev = skp_ref[...], svp_ref[...]
        else:
            kprev, vprev = sk_ref[(cb - 1) * w:cb * w, :], sv_ref[(cb - 1) * w:cb * w, :]
        kcat = jnp.concatenate([kprev, sk_ref[rows, :]], axis=0)
        vcat = jnp.concatenate([vprev, sv_ref[rows, :]], axis=0)
        outs = []
        for kv in range(SWA_KV_HEADS):
            keep = (lane < HEAD_DIM) if kv == 0 else (lane >= HEAD_DIM)
            qs = [jnp.where(keep, sq_ref[rows, gi * LANES:(gi + 1) * LANES], jnp.zeros((), BF16)) for gi in range(g)]
            qstack = jnp.concatenate(qs, axis=0)
            s = lax.dot_general(qstack, kcat, nt, preferred_element_type=F32) + bias_ref[kv]
            if cb == 0:
                s = s + jnp.where(col < w, jnp.where(i == 0, NEG_INF, 0.0), 0.0)
            sink = jnp.concatenate([jnp.full((w, 1), sinks_ref[kv * g + gi], F32) for gi in range(g)], axis=0)
            m = jnp.maximum(jnp.max(s, axis=-1, keepdims=True), sink)
            p = jnp.exp(s - m)
            denom = jnp.sum(p, axis=-1, keepdims=True) + jnp.exp(sink - m)
            outs.append(jnp.dot(p.astype(BF16), vcat, preferred_element_type=F32) / denom)
        for gi in range(g):
            o_ref[rows, gi * LANES:(gi + 1) * LANES] = jnp.where(
                lane < HEAD_DIM, outs[0][gi * w:(gi + 1) * w, :], outs[1][gi * w:(gi + 1) * w, :]).astype(o_ref.dtype)


def _swa(sq, sk, sv, sinks, rel_bias, batch, seq):
    w = WINDOW
    dh = HEAD_DIM
    blocks_per_tile = min(4, seq // w)
    t = blocks_per_tile * w
    assert seq % t == 0
    sq3 = sq.reshape(batch, seq, SWA_HEADS * dh)
    sk3 = sk.reshape(batch, seq, SWA_KV_HEADS * dh)
    sv3 = sv.reshape(batch, seq, SWA_KV_HEADS * dh)
    bucket = jnp.asarray(_t5_bucket_table())
    cur = lambda b, i: (b, i, 0)
    prev = lambda b, i: (b, jnp.maximum(i * blocks_per_tile - 1, 0), 0)
    smem = pl.BlockSpec(memory_space=pltpu.SMEM)
    out = pl.pallas_call(
        functools.partial(_swa_kernel, blocks_per_tile=blocks_per_tile),
        grid=(batch, seq // t),
        in_specs=[
            smem, smem,
            pl.BlockSpec((w, 2 * w), lambda b, i: (0, 0)),
            pl.BlockSpec((None, t, SWA_HEADS * dh), cur),
            pl.BlockSpec((None, t, SWA_KV_HEADS * dh), cur),
            pl.BlockSpec((None, t, SWA_KV_HEADS * dh), cur),
            pl.BlockSpec((None, w, SWA_KV_HEADS * dh), prev),
            pl.BlockSpec((None, w, SWA_KV_HEADS * dh), prev),
        ],
        out_specs=pl.BlockSpec((None, t, SWA_HEADS * dh), cur),
        out_shape=jax.ShapeDtypeStruct((batch, seq, SWA_HEADS * dh), BF16),
        scratch_shapes=[pltpu.VMEM((SWA_KV_HEADS, SWA_GROUP * w, 2 * w), F32)],
        compiler_params=pltpu.CompilerParams(dimension_semantics=("arbitrary", "arbitrary"),
                                             vmem_limit_bytes=VMEM_LIMIT_BYTES),
        name="swa",
    )(rel_bias, sinks, bucket, sq3, sk3, sv3, sk3, sv3)
    return out.reshape(batch * seq, SWA_HEADS * dh)


def _outproj_kernel(x_ref, fa_ref, sb_ref, wa_ref, wb_ref, g_ref, wrh_ref, wrl_ref, br_ref, lstrict_ref,
                    h_ref, xs_ref, meta_ref, gates_ref, counts_ref, carry_ref, *, n_experts):
    i = pl.program_id(0)
    tm = x_ref.shape[0]

    @pl.when(i == 0)
    def _():
        carry_ref[...] = jnp.zeros_like(carry_ref)

    h = (x_ref[...] + jnp.dot(fa_ref[...], wa_ref[...], preferred_element_type=F32)
         + jnp.dot(sb_ref[...], wb_ref[...], preferred_element_type=F32))
    h_ref[...] = h
    ms = jnp.mean(h * h, axis=-1, keepdims=True)
    hn = h * lax.rsqrt(ms + EPS) * g_ref[...]
    xs_ref[...] = _pack_bf16_pairs(hn)

    hn_hi = hn.astype(BF16)
    hn_lo = (hn - hn_hi.astype(F32)).astype(BF16)
    wrh = wrh_ref[...]
    logits = (jnp.dot(hn_hi, wrh, preferred_element_type=F32) + jnp.dot(hn_lo, wrh, preferred_element_type=F32)
              + jnp.dot(hn_hi, wrl_ref[...], preferred_element_type=F32) + br_ref[...])
    lane = _lane_iota((tm, LANES))
    lane_f = lane.astype(F32)
    l = jnp.where(lane < n_experts, logits, NEG_INF)
    vals, idxs = [], []
    member = jnp.zeros((tm, LANES), F32)
    for _ in range(TOP_K):
        mk = jnp.max(l, axis=-1, keepdims=True)
        ik = jnp.min(jnp.where(l == mk, lane_f, float(LANES)), axis=-1, keepdims=True)
        sel = lane_f == ik
        l = jnp.where(sel, NEG_INF, l)
        member = jnp.where(sel, 1.0, member)
        vals.append(mk)
        idxs.append(ik)
    es = [jnp.exp(v - vals[0]) for v in vals]
    tot = es[0] + es[1] + es[2] + es[3]
    gates = [e / tot for e in es]

    rank = jnp.dot(lstrict_ref[...], member.astype(BF16), preferred_element_type=F32) + carry_ref[0:1, :]
    carry_new = rank[tm - 1:tm, :] + member[tm - 1:tm, :]
    carry_ref[0:1, :] = carry_new
    counts_ref[...] = jnp.broadcast_to(carry_new, counts_ref.shape)
    ranks = [jnp.sum(jnp.where(lane_f == ik, rank, 0.0), axis=-1, keepdims=True) for ik in idxs]

    meta = jnp.zeros((tm, LANES), F32)
    gt = jnp.zeros((tm, LANES), F32)
    for k in range(TOP_K):
        meta = jnp.where(lane == k, idxs[k], meta)
        meta = jnp.where(lane == TOP_K + k, ranks[k], meta)
        gt = jnp.where(lane == k, gates[k], gt)
    meta_ref[...] = meta.astype(jnp.int32)
    gates_ref[...] = gt


def _outproj(x2d, fa, sb, w_out, norm_moe, w_router, b_router):
    n, d = x2d.shape
    n_experts = w_router.shape[1]
    assert n_experts <= LANES
    tm = min(512, n)
    assert n % tm == 0
    nfa = FOX_HEADS * HEAD_DIM
    wa = w_out[:nfa].astype(BF16)
    wb = w_out[nfa:].reshape(SWA_KV_HEADS, SWA_GROUP, HEAD_DIM, d).transpose(1, 0, 2, 3).reshape(SWA_HEADS * HEAD_DIM, d)
    wb = wb.astype(BF16)
    wr = jnp.zeros((d, LANES), F32).at[:, :n_experts].set(w_router)
    wr_hi = wr.astype(BF16)
    wr_lo = (wr - wr_hi.astype(F32)).astype(BF16)
    br = jnp.zeros((1, LANES), F32).at[0, :n_experts].set(b_router)
    lstrict = jnp.asarray(np.tril(np.ones((tm, tm), np.float32), -1), BF16)
    const = lambda i: (0, 0)
    row = lambda i: (i, 0)
    return pl.pallas_call(
        functools.partial(_outproj_kernel, n_experts=n_experts),
        grid=(n // tm,),
        in_specs=[
            pl.BlockSpec((tm, d), row),
            pl.BlockSpec((tm, nfa), row),
            pl.BlockSpec((tm, SWA_HEADS * HEAD_DIM), row),
            pl.BlockSpec((nfa, d), const),
            pl.BlockSpec((SWA_HEADS * HEAD_DIM, d), const),
            pl.BlockSpec((1, d), const),
            pl.BlockSpec((d, LANES), const),
            pl.BlockSpec((d, LANES), const),
            pl.BlockSpec((1, LANES), const),
            pl.BlockSpec((tm, tm), const),
        ],
        out_specs=[
            pl.BlockSpec((tm, d), row),
            pl.BlockSpec((tm, d // 2), row),
            pl.BlockSpec((tm, LANES), row),
            pl.BlockSpec((tm, LANES), row),
            pl.BlockSpec((8, LANES), const),
        ],
        out_shape=(
            jax.ShapeDtypeStruct((n, d), F32),
            jax.ShapeDtypeStruct((n, d // 2), jnp.uint32),
            jax.ShapeDtypeStruct((n, LANES), jnp.int32),
            jax.ShapeDtypeStruct((n, LANES), F32),
            jax.ShapeDtypeStruct((8, LANES), F32),
        ),
        scratch_shapes=[pltpu.VMEM((8, LANES), F32)],
        compiler_params=pltpu.CompilerParams(dimension_semantics=("arbitrary",), vmem_limit_bytes=VMEM_LIMIT_BYTES),
        name="outproj",
    )(x2d, fa, sb, wa, wb, norm_moe.reshape(1, d), wr_hi, wr_lo, br, lstrict)


def _dispatch_kernel(pos_ref, x_ref, init_ref, out_ref, sem):
    del init_ref
    tm = x_ref.shape[0]

    def row_copy(t, p):
        return pltpu.make_async_copy(x_ref.at[pl.ds(t, 1), :], out_ref.at[pl.ds(p, 1), :], sem)

    def start(t, c):
        for k in range(TOP_K):
            row_copy(t, pos_ref[0, 0, t * TOP_K + k]).start()
        return c

    def wait(t, c):
        for k in range(TOP_K):
            row_copy(t, pos_ref[0, 0, t * TOP_K + k]).wait()
        return c

    lax.fori_loop(0, tm, start, 0)
    lax.fori_loop(0, tm, wait, 0)


def _dispatch(xs_tok, pos, rows_total):
    n, w = xs_tok.shape
    tm = min(512, n)
    pos3 = pos.reshape(n // tm, 1, tm * TOP_K)
    init = jnp.zeros((rows_total, w), xs_tok.dtype)
    return pl.pallas_call(
        _dispatch_kernel,
        grid=(n // tm,),
        in_specs=[
            pl.BlockSpec((1, 1, tm * TOP_K), lambda i: (i, 0, 0), memory_space=pltpu.SMEM),
            pl.BlockSpec((tm, w), lambda i: (i, 0)),
            pl.BlockSpec(memory_space=pl.ANY),
        ],
        out_specs=pl.BlockSpec(memory_space=pl.ANY),
        out_shape=jax.ShapeDtypeStruct((rows_total, w), xs_tok.dtype),
        scratch_shapes=[pltpu.SemaphoreType.DMA(())],
        input_output_aliases={2: 0},
        compiler_params=pltpu.CompilerParams(dimension_semantics=("arbitrary",), vmem_limit_bytes=VMEM_LIMIT_BYTES,
                                             has_side_effects=True),
        name="dispatch",
    )(pos3, xs_tok, init)


def _experts_kernel(be_ref, bv_ref, x_ref, wug_ref, bug_ref, wd_ref, bd_ref, y_ref, wug_bf, wd_bf):
    i = pl.program_id(0)
    f = wd_ref.shape[0]
    prev = be_ref[jnp.maximum(i - 1, 0)]

    @pl.when((i == 0) | (be_ref[i] != prev))
    def _():
        wug_bf[...] = wug_ref[...].astype(BF16)
        wd_bf[...] = wd_ref[...].astype(BF16)

    @pl.when(bv_ref[i] > 0)
    def _():
        first, second = _unpack_bf16_pairs(x_ref[...])
        x = jnp.concatenate([first.astype(BF16), second.astype(BF16)], axis=1)
        gu = jnp.dot(x, wug_bf[...], preferred_element_type=F32) + bug_ref[...]
        gate = jnp.minimum(gu[:, :f], SWIGLU_LIMIT)
        up = jnp.clip(gu[:, f:], -SWIGLU_LIMIT, SWIGLU_LIMIT)
        act = (up + 1.0) * (gate * (1.0 / (1.0 + jnp.exp(-SWIGLU_ALPHA * gate))))
        y = jnp.dot(act.astype(BF16), wd_bf[...], preferred_element_type=F32) + bd_ref[...]
        y_ref[...] = _pack_bf16_pairs(y)

    @pl.when(bv_ref[i] <= 0)
    def _():
        y_ref[...] = jnp.zeros_like(y_ref)


def _experts(xs_sorted, blk_exp, blk_valid, w_up_gate, b_up_gate, w_down, b_down, tr):
    rows_total, w = xs_sorted.shape
    e, d, f2 = w_up_gate.shape
    f = f2 // 2
    grid_spec = pltpu.PrefetchScalarGridSpec(
        num_scalar_prefetch=2,
        grid=(rows_total // tr,),
        in_specs=[
            pl.BlockSpec((tr, w), lambda i, be, bv: (i, 0)),
            pl.BlockSpec((None, d, f2), lambda i, be, bv: (be[i], 0, 0)),
            pl.BlockSpec((None, 1, f2), lambda i, be, bv: (be[i], 0, 0)),
            pl.BlockSpec((None, f, d), lambda i, be, bv: (be[i], 0, 0)),
            pl.BlockSpec((None, 1, d), lambda i, be, bv: (be[i], 0, 0)),
        ],
        out_specs=pl.BlockSpec((tr, d // 2), lambda i, be, bv: (i, 0)),
        scratch_shapes=[pltpu.VMEM((d, f2), BF16), pltpu.VMEM((f, d), BF16)],
    )
    return pl.pallas_call(
        _experts_kernel,
        grid_spec=grid_spec,
        out_shape=jax.ShapeDtypeStruct((rows_total, d // 2), jnp.uint32),
        compiler_params=pltpu.CompilerParams(dimension_semantics=("arbitrary",), vmem_limit_bytes=VMEM_LIMIT_BYTES),
        name="experts",
    )(blk_exp, blk_valid, xs_sorted, w_up_gate, b_up_gate.reshape(e, 1, f2), w_down, b_down.reshape(e, 1, d))


def _combine_kernel(pos_ref, posn_ref, h_ref, gates_ref, ys_ref, o_ref, buf, sems):
    i = pl.program_id(0)
    n_steps = pl.num_programs(0)
    tm = h_ref.shape[0]
    slot = i % 2

    def row_copy(p_ref, t, k, s):
        p = p_ref[0, 0, t * TOP_K + k]
        return pltpu.make_async_copy(ys_ref.at[pl.ds(p, 1), :], buf.at[s, k, pl.ds(t, 1), :], sems.at[s, k])

    def issue(p_ref, s):
        def body(t, c):
            for k in range(TOP_K):
                row_copy(p_ref, t, k, s).start()
            return c
        lax.fori_loop(0, tm, body, 0)

    @pl.when(i == 0)
    def _():
        issue(pos_ref, 0)

    @pl.when(i + 1 < n_steps)
    def _():
        issue(posn_ref, 1 - slot)

    def wait_body(t, c):
        for k in range(TOP_K):
            row_copy(pos_ref, t, k, slot).wait()
        return c
    lax.fori_loop(0, tm, wait_body, 0)

    acc = h_ref[...]
    gates = gates_ref[...]
    for k in range(TOP_K):
        first, second = _unpack_bf16_pairs(buf[slot, k])
        acc = acc + gates[:, k:k + 1] * jnp.concatenate([first, second], axis=1)
    o_ref[...] = acc


def _combine(h, gates, pos, ys):
    n, d = h.shape
    tm = min(256, n)
    nt = n // tm
    pos3 = pos.reshape(nt, 1, tm * TOP_K)
    smem_blk = lambda imap: pl.BlockSpec((1, 1, tm * TOP_K), imap, memory_space=pltpu.SMEM)
    return pl.pallas_call(
        _combine_kernel,
        grid=(nt,),
        in_specs=[
            smem_blk(lambda i: (i, 0, 0)),
            smem_blk(lambda i: (jnp.minimum(i + 1, nt - 1), 0, 0)),
            pl.BlockSpec((tm, d), lambda i: (i, 0)),
            pl.BlockSpec((tm, LANES), lambda i: (i, 0)),
            pl.BlockSpec(memory_space=pl.ANY),
        ],
        out_specs=pl.BlockSpec((tm, d), lambda i: (i, 0)),
        out_shape=jax.ShapeDtypeStruct((n, d), F32),
        scratch_shapes=[pltpu.VMEM((2, TOP_K, tm, d // 2), jnp.uint32), pltpu.SemaphoreType.DMA((2, TOP_K))],
        compiler_params=pltpu.CompilerParams(dimension_semantics=("arbitrary",), vmem_limit_bytes=VMEM_LIMIT_BYTES),
        name="combine",
    )(pos3, pos3, h, gates, ys)


EXPERT_TILE = 256


def _layer(h2d, batch, seq, norm_mix, w_in, b_forget, q_norm_fox, k_norm_fox, q_norm_swa, k_norm_swa, sinks, rel_bias,
           w_out, norm_moe, w_router, b_router, w_up_gate, b_up_gate, w_down, b_down):
    n, d = h2d.shape
    n_experts = w_router.shape[1]
    qa, ka, fv, sq, sk, sv = _proj(h2d, norm_mix, w_in, b_forget, q_norm_fox, k_norm_fox, q_norm_swa, k_norm_swa, seq)
    fa = _fox(qa, ka, fv, batch, seq)
    sb = _swa(sq, sk, sv, sinks, rel_bias, batch, seq)
    h, xs_tok, meta, gates, counts = _outproj(h2d, fa, sb, w_out, norm_moe, w_router, b_router)

    tr = min(EXPERT_TILE, n)
    counts_i = counts[0, :n_experts].astype(jnp.int32)
    padded = (counts_i + tr - 1) // tr * tr
    pend = jnp.cumsum(padded)
    pstart = pend - padded
    idx = meta[:, :TOP_K]
    rank = meta[:, TOP_K:2 * TOP_K]
    pos = pstart[idx] + rank
    rows_total = n * TOP_K + n_experts * tr
    n_tiles = rows_total // tr
    blk_start = jnp.arange(n_tiles, dtype=jnp.int32) * tr
    blk_exp = jnp.minimum(jnp.sum(pend[None, :] <= blk_start[:, None], axis=1), n_experts - 1).astype(jnp.int32)
    blk_valid = jnp.clip(pstart[blk_exp] + counts_i[blk_exp] - blk_start, 0, tr).astype(jnp.int32)

    xs_sorted = _dispatch(xs_tok, pos.astype(jnp.int32), rows_total)
    ys = _experts(xs_sorted, blk_exp, blk_valid, w_up_gate, b_up_gate, w_down, b_down, tr)
    return _combine(h, gates, pos.astype(jnp.int32), ys)


def kernel(x, norm_mix, w_in, b_forget, q_norm_fox, k_norm_fox, q_norm_swa, k_norm_swa, sinks, rel_bias, w_out, norm_moe,
           w_router, b_router, w_up_gate, b_up_gate, w_down, b_down):
    batch, seq, d = x.shape
    h = x.reshape(batch * seq, d)
    for l in range(norm_mix.shape[0]):
        h = _layer(h, batch, seq, norm_mix[l], w_in[l], b_forget[l], q_norm_fox[l], k_norm_fox[l], q_norm_swa[l],
                   k_norm_swa[l], sinks[l], rel_bias, w_out[l], norm_moe[l], w_router[l], b_router[l], w_up_gate[l],
                   b_up_gate[l], w_down[l], b_down[l])
    return h.reshape(batch, seq, d)
```

```python
import functools
import math

import numpy as np
import jax
import jax.numpy as jnp
from jax import lax
from jax.experimental import pallas as pl
from jax.experimental.pallas import tpu as pltpu

HEAD_DIM = 64
FOX_HEADS = 8
SWA_HEADS = 8
SWA_KV_HEADS = 2
SWA_GROUP = SWA_HEADS // SWA_KV_HEADS
WINDOW = 128
N_BUCKETS = 32
MAX_DISTANCE = 128
TOP_K = 4
SWIGLU_LIMIT = 7.0
SWIGLU_ALPHA = 1.702
EPS = 1e-6

LANES = 128
VMEM_LIMIT_BYTES = 56 * 1024 * 1024
NEG_INF = float("-inf")
LOG2E = math.log2(math.e)

F32 = jnp.float32
BF16 = jnp.bfloat16


def _lane_iota(shape):
    return lax.broadcasted_iota(jnp.int32, shape, len(shape) - 1)


def _split3_bf16(v):
    hi = v.astype(BF16)
    r1 = v - hi.astype(F32)
    mid = r1.astype(BF16)
    lo = (r1 - mid.astype(F32)).astype(BF16)
    return hi, mid, lo


def _pack_bf16_pairs(v):
    w = v.shape[1] // 2
    u = pltpu.bitcast(v.astype(BF16).astype(F32), jnp.uint32)
    return u[:, :w] | (u[:, w:] >> 16)


def _unpack_bf16_pairs(u):
    first = pltpu.bitcast(u & jnp.uint32(0xFFFF0000), F32)
    second = pltpu.bitcast(u << 16, F32)
    return first, second


CUM_CHUNK = 128


def _proj_kernel(x_ref, g_ref, wqa_ref, wka_ref, wrest_ref, bf_ref, gqa_ref, gka_ref, gsq_ref, gsk_ref, tri_ref,
                 qa_ref, ka_ref, fv_ref, sq_ref, sk_ref, sv_ref, carry_ref, *, tiles_per_seq):
    i = pl.program_id(0)
    tm = x_ref.shape[0]
    nh = FOX_HEADS
    scale = HEAD_DIM ** -0.5

    @pl.when(i % tiles_per_seq == 0)
    def _():
        carry_ref[...] = jnp.zeros_like(carry_ref)

    x = x_ref[...]
    ms = jnp.mean(x * x, axis=-1, keepdims=True)
    hn = (x * lax.rsqrt(ms + EPS) * g_ref[...]).astype(BF16)

    qa = jnp.dot(hn, wqa_ref[...], preferred_element_type=F32)
    ka = jnp.dot(hn, wka_ref[...], preferred_element_type=F32)
    rest = jnp.dot(hn, wrest_ref[...], preferred_element_type=F32)

    lane_w = _lane_iota((tm, nh * LANES)) % LANES
    upper = lane_w >= HEAD_DIM

    fl = qa + bf_ref[...]
    ls = jnp.minimum(fl, 0.0) - jnp.log(1.0 + jnp.exp(-jnp.abs(fl)))
    ls = jnp.where(upper, ls, 0.0)
    tri = tri_ref[...]
    carry = carry_ref[0:1, :]
    c_chunks = []
    for r in range(tm // CUM_CHUNK):
        hi, mid, lo = _split3_bf16(ls[r * CUM_CHUNK:(r + 1) * CUM_CHUNK, :])
        cs = jnp.dot(tri, jnp.concatenate([hi, mid, lo], axis=0), preferred_element_type=F32) + carry
        carry = cs[CUM_CHUNK - 1:CUM_CHUNK, :]
        c_chunks.append(cs)
    carry_ref[0:1, :] = carry
    c = jnp.concatenate(c_chunks, axis=0) * LOG2E

    c1 = c.astype(BF16).astype(F32)
    c2 = (c - c1).astype(BF16).astype(F32)
    c3 = c - c1 - c2

    def head_norm(t, g_row):
        outs = []
        for h in range(t.shape[1] // LANES):
            th = t[:, h * LANES:(h + 1) * LANES]
            lo_mask = _lane_iota(th.shape) < HEAD_DIM
            ss = jnp.sum(jnp.where(lo_mask, th * th, 0.0), axis=-1, keepdims=True)
            outs.append(th * lax.rsqrt(ss * (1.0 / HEAD_DIM) + EPS))
        return jnp.concatenate(outs, axis=1) * g_row

    qn = head_norm(qa, gqa_ref[...]) * (scale * LOG2E)
    kn = head_norm(ka, gka_ref[...])

    one = jnp.float32(1.0)
    zero = jnp.float32(0.0)
    q_aug = jnp.where(lane_w == 64, c1, jnp.where(lane_w == 65, c2, jnp.where(lane_w == 66, c3,
                      jnp.where(lane_w < 70, one, zero))))
    k_aug = jnp.where(lane_w < 67, one, jnp.where(lane_w == 67, -c1, jnp.where(lane_w == 68, -c2,
                      jnp.where(lane_w == 69, -c3, zero))))
    qa_ref[...] = jnp.where(upper, q_aug, qn).astype(BF16)
    ka_ref[...] = jnp.where(upper, k_aug, kn).astype(BF16)

    nfv = FOX_HEADS * HEAD_DIM
    nsq = SWA_HEADS * HEAD_DIM
    nsk = SWA_KV_HEADS * HEAD_DIM
    fv_ref[...] = rest[:, :nfv].astype(BF16)

    def half_norm(t, g_row):
        outs = []
        for j in range(t.shape[1] // LANES):
            tj = t[:, j * LANES:(j + 1) * LANES]
            lo_mask = _lane_iota(tj.shape) < HEAD_DIM
            sq = tj * tj
            s_lo = jnp.sum(jnp.where(lo_mask, sq, 0.0), axis=-1, keepdims=True)
            s_hi = jnp.sum(jnp.where(lo_mask, 0.0, sq), axis=-1, keepdims=True)
            r = jnp.where(lo_mask, lax.rsqrt(s_lo * (1.0 / HEAD_DIM) + EPS), lax.rsqrt(s_hi * (1.0 / HEAD_DIM) + EPS))
            outs.append(tj * r)
        return jnp.concatenate(outs, axis=1) * g_row

    sq_ref[...] = (half_norm(rest[:, nfv:nfv + nsq], gsq_ref[...]) * scale).astype(BF16)
    sk_ref[...] = half_norm(rest[:, nfv + nsq:nfv + nsq + nsk], gsk_ref[...]).astype(BF16)
    sv_ref[...] = rest[:, nfv + nsq + nsk:].astype(BF16)


def _proj(x2d, norm_mix, w_in, b_forget, q_norm_fox, k_norm_fox, q_norm_swa, k_norm_swa, seq):
    n, d = x2d.shape
    nh, dh = FOX_HEADS, HEAD_DIM
    tm = min(512, seq)
    assert seq % tm == 0 and tm % CUM_CHUNK == 0
    points = np.cumsum([nh * dh, nh * dh, nh * dh, nh, SWA_HEADS * dh, SWA_KV_HEADS * dh, SWA_KV_HEADS * dh])
    wfq, wfk, wfv, wff, wsq, wsk, wsv = jnp.split(w_in, [int(p) for p in points[:-1]], axis=1)

    wfq = wfq.reshape(d, nh, dh)
    wff_rep = jnp.broadcast_to(wff[:, :, None], (d, nh, dh))
    wqa = jnp.concatenate([wfq, wff_rep], axis=2).reshape(d, nh * LANES).astype(BF16)
    wka = jnp.concatenate([wfk.reshape(d, nh, dh), jnp.zeros((d, nh, dh), w_in.dtype)], axis=2)
    wka = wka.reshape(d, nh * LANES).astype(BF16)
    wsq = wsq.reshape(d, SWA_KV_HEADS, SWA_GROUP, dh).transpose(0, 2, 1, 3).reshape(d, SWA_HEADS * dh)
    wrest = jnp.concatenate([wfv, wsq, wsk, wsv], axis=1).astype(BF16)

    zeros_h = jnp.zeros((nh, dh), F32)
    bf_row = jnp.concatenate([zeros_h, jnp.broadcast_to(b_forget[:, None], (nh, dh))], axis=1).reshape(1, nh * LANES)
    gqa = jnp.concatenate([jnp.broadcast_to(q_norm_fox[None], (nh, dh)), zeros_h], axis=1).reshape(1, nh * LANES)
    gka = jnp.concatenate([jnp.broadcast_to(k_norm_fox[None], (nh, dh)), zeros_h], axis=1).reshape(1, nh * LANES)
    gsq = jnp.tile(q_norm_swa, SWA_HEADS).reshape(1, SWA_HEADS * dh)
    gsk = jnp.tile(k_norm_swa, SWA_KV_HEADS).reshape(1, SWA_KV_HEADS * dh)
    tri = np.tril(np.ones((CUM_CHUNK, CUM_CHUNK), np.float32))
    tri3 = jnp.asarray(np.concatenate([tri, tri, tri], axis=1), BF16)

    nrest = wrest.shape[1]
    const = lambda i: (0, 0)
    row = lambda i: (i, 0)
    out_shapes = (
        jax.ShapeDtypeStruct((n, nh * LANES), BF16),
        jax.ShapeDtypeStruct((n, nh * LANES), BF16),
        jax.ShapeDtypeStruct((n, nh * dh), BF16),
        jax.ShapeDtypeStruct((n, SWA_HEADS * dh), BF16),
        jax.ShapeDtypeStruct((n, SWA_KV_HEADS * dh), BF16),
        jax.ShapeDtypeStruct((n, SWA_KV_HEADS * dh), BF16),
    )
    return pl.pallas_call(
        functools.partial(_proj_kernel, tiles_per_seq=seq // tm),
        grid=(n // tm,),
        in_specs=[
            pl.BlockSpec((tm, d), row),
            pl.BlockSpec((1, d), const),
            pl.BlockSpec((d, nh * LANES), const),
            pl.BlockSpec((d, nh * LANES), const),
            pl.BlockSpec((d, nrest), const),
            pl.BlockSpec((1, nh * LANES), const),
            pl.BlockSpec((1, nh * LANES), const),
            pl.BlockSpec((1, nh * LANES), const),
            pl.BlockSpec((1, SWA_HEADS * dh), const),
            pl.BlockSpec((1, SWA_KV_HEADS * dh), const),
            pl.BlockSpec((CUM_CHUNK, 3 * CUM_CHUNK), const),
        ],
        out_specs=[
            pl.BlockSpec((tm, nh * LANES), row),
            pl.BlockSpec((tm, nh * LANES), row),
            pl.BlockSpec((tm, nh * dh), row),
            pl.BlockSpec((tm, SWA_HEADS * dh), row),
            pl.BlockSpec((tm, SWA_KV_HEADS * dh), row),
            pl.BlockSpec((tm, SWA_KV_HEADS * dh), row),
        ],
        out_shape=out_shapes,
        scratch_shapes=[pltpu.VMEM((8, nh * LANES), F32)],
        compiler_params=pltpu.CompilerParams(dimension_semantics=("arbitrary",), vmem_limit_bytes=VMEM_LIMIT_BYTES),
        name="proj",
    )(x2d, norm_mix.reshape(1, d), wqa, wka, wrest, bf_row, gqa, gka, gsq, gsk, tri3)


FOX_BLOCK = 512


def _fox_kernel(qa_ref, ka_ref, v_ref, o_ref, s0_sc, s1_sc, m_sc, l_sc, acc_sc, *, blk):
    i = pl.program_id(2)
    nt = (((1,), (1,)), ((), ()))
    row = lax.broadcasted_iota(jnp.int32, (blk, blk), 0)
    col = lax.broadcasted_iota(jnp.int32, (blk, blk), 1)
    causal = col <= row
    s_bufs = (s0_sc, s1_sc)

    def lane_tiles(s):
        return [s[:, c * LANES:(c + 1) * LANES] for c in range(s.shape[1] // LANES)]

    def row_max(tiles):
        m = jnp.max(functools.reduce(jnp.maximum, tiles), axis=-1, keepdims=True)
        return jnp.broadcast_to(m, (blk, LANES))

    def row_sum(tiles):
        t = jnp.sum(functools.reduce(jnp.add, tiles), axis=-1, keepdims=True)
        return jnp.broadcast_to(t, (blk, LANES))

    def produce_scores(j, slot):
        start = pl.multiple_of(j * blk, blk)
        for hh in range(2):
            q = qa_ref[:, hh * LANES:(hh + 1) * LANES]
            k = ka_ref[pl.ds(start, blk), hh * LANES:(hh + 1) * LANES]
            s_bufs[slot][hh] = lax.dot_general(q, k, nt, preferred_element_type=F32)

    def consume_scores(j, slot, masked):
        v = v_ref[pl.ds(pl.multiple_of(j * blk, blk), blk), :]
        for hh in range(2):
            s = s_bufs[slot][hh]
            if masked:
                s = jnp.where(causal, s, NEG_INF)
            tiles = lane_tiles(s)
            m = m_sc[hh]
            m_new = jnp.maximum(m, row_max(tiles))
            alpha = jnp.exp2(m - m_new)
            p_tiles = [jnp.exp2(t - m_new) for t in tiles]
            p = jnp.concatenate(p_tiles, axis=1).astype(BF16)
            m_sc[hh] = m_new
            l_sc[hh] = alpha * l_sc[hh] + row_sum(p_tiles)
            acc_sc[hh] = alpha * acc_sc[hh] + jnp.dot(p, v, preferred_element_type=F32)

    m_sc[...] = jnp.full(m_sc.shape, NEG_INF, F32)
    l_sc[...] = jnp.zeros(l_sc.shape, F32)
    acc_sc[...] = jnp.zeros(acc_sc.shape, F32)
    produce_scores(0, 0)

    def pair(jj, c):
        j = 2 * jj
        produce_scores(j + 1, 1)
        consume_scores(j, 0, masked=False)
        produce_scores(j + 2, 0)
        consume_scores(j + 1, 1, masked=False)
        return c

    lax.fori_loop(0, i // 2, pair, 0)

    @pl.when(i % 2 == 0)
    def _():
        consume_scores(i, 0, masked=True)

    @pl.when(i % 2 == 1)
    def _():
        produce_scores(i, 1)
        consume_scores(i - 1, 0, masked=False)
        consume_scores(i, 1, masked=True)

    out0 = acc_sc[0] / l_sc[0]
    out1 = acc_sc[1] / l_sc[1]
    lane = _lane_iota(out0.shape)
    o_ref[...] = jnp.where(lane < HEAD_DIM, out0, out1).astype(o_ref.dtype)


def _fox(qa, ka, fv, batch, seq):
    nh, dh = FOX_HEADS, HEAD_DIM
    blk = min(FOX_BLOCK, seq)
    assert seq % blk == 0
    qa3 = qa.reshape(batch, seq, nh * LANES)
    ka3 = ka.reshape(batch, seq, nh * LANES)
    fv3 = fv.reshape(batch, seq, nh * dh)
    out = pl.pallas_call(
        functools.partial(_fox_kernel, blk=blk),
        grid=(batch, nh // 2, seq // blk),
        in_specs=[
            pl.BlockSpec((None, blk, 2 * LANES), lambda b, j, i: (b, i, j)),
            pl.BlockSpec((None, seq, 2 * LANES), lambda b, j, i: (b, 0, j)),
            pl.BlockSpec((None, seq, LANES), lambda b, j, i: (b, 0, j)),
        ],
        out_specs=pl.BlockSpec((None, blk, LANES), lambda b, j, i: (b, i, j)),
        out_shape=jax.ShapeDtypeStruct((batch, seq, nh * dh), BF16),
        scratch_shapes=[
            pltpu.VMEM((2, blk, blk), F32),
            pltpu.VMEM((2, blk, blk), F32),
            pltpu.VMEM((2, blk, LANES), F32),
            pltpu.VMEM((2, blk, LANES), F32),
            pltpu.VMEM((2, blk, LANES), F32),
        ],
        compiler_params=pltpu.CompilerParams(dimension_semantics=("arbitrary", "arbitrary", "arbitrary"),
                                             vmem_limit_bytes=VMEM_LIMIT_BYTES),
        name="fox",
    )(qa3, ka3, fv3)
    return out.reshape(batch * seq, nh * dh)


def _t5_bucket_table():
    qi = np.arange(WINDOW)[:, None]
    kj = np.arange(2 * WINDOW)[None, :]
    dist = qi + WINDOW - kj
    n = np.maximum(dist, 0)
    max_exact = N_BUCKETS // 2
    nf = np.maximum(n, 1).astype(np.float64)
    large = max_exact + (np.log(nf / max_exact) / math.log(MAX_DISTANCE / max_exact) * (N_BUCKETS - max_exact)).astype(np.int64)
    large = np.minimum(large, N_BUCKETS - 1)
    bucket = np.where(n < max_exact, n, large)
    valid = (dist >= 0) & (dist < WINDOW)
    return np.where(valid, bucket, -1).astype(np.int32)


def _swa_kernel(rb_ref, sinks_ref, bucket_ref, sq_ref, sk_ref, sv_ref, skp_ref, svp_ref, o_ref, bias_ref, *, blocks_per_tile):
    b = pl.program_id(0)
    i = pl.program_id(1)
    w = WINDOW
    g = SWA_GROUP

    @pl.when((b == 0) & (i == 0))
    def _():
        bucket = bucket_ref[...]
        for h in range(SWA_HEADS):
            acc = jnp.full((w, 2 * w), NEG_INF, F32)
            for bk in range(N_BUCKETS):
                acc = jnp.where(bucket == bk, rb_ref[bk, h], acc)
            kv, gi = divmod(h, g)
            bias_ref[kv, gi * w:(gi + 1) * w, :] = acc

    nt = (((1,), (1,)), ((), ()))
    lane = _lane_iota((w, LANES))
    col = _lane_iota((g * w, 2 * w))
    for cb in range(blocks_per_tile):
        rows = slice(cb * w, (cb + 1) * w)
        if cb == 0:
            kprev, vprev = skp_ref[...], svp_ref[...]
        else:
            kprev, vprev = sk_ref[(cb - 1) * w:cb * w, :], sv_ref[(cb - 1) * w:cb * w, :]
        kcat = jnp.concatenate([kprev, sk_ref[rows, :]], axis=0)
        vcat = jnp.concatenate([vprev, sv_ref[rows, :]], axis=0)
        outs = []
        for kv in range(SWA_KV_HEADS):
            keep = (lane < HEAD_DIM) if kv == 0 else (lane >= HEAD_DIM)
            qs = [jnp.where(keep, sq_ref[rows, gi * LANES:(gi + 1) * LANES], jnp.zeros((), BF16)) for gi in range(g)]
            qstack = jnp.concatenate(qs, axis=0)
            s = lax.dot_general(qstack, kcat, nt, preferred_element_type=F32) + bias_ref[kv]
            if cb == 0:
                s = s + jnp.where(col < w, jnp.where(i == 0, NEG_INF, 0.0), 0.0)
            sink = jnp.concatenate([jnp.full((w, 1), sinks_ref[kv * g + gi], F32) for gi in range(g)], axis=0)
            m = jnp.maximum(jnp.max(s, axis=-1, keepdims=True), sink)
            p = jnp.exp(s - m)
            denom = jnp.sum(p, axis=-1, keepdims=True) + jnp.exp(sink - m)
            outs.append(jnp.dot(p.astype(BF16), vcat, preferred_element_type=F32) / denom)
        for gi in range(g):
            o_ref[rows, gi * LANES:(gi + 1) * LANES] = jnp.where(
                lane < HEAD_DIM, outs[0][gi * w:(gi + 1) * w, :], outs[1][gi * w:(gi + 1) * w, :]).astype(o_ref.dtype)


def _swa(sq, sk, sv, sinks, rel_bias, batch, seq):
    w = WINDOW
    dh = HEAD_DIM
    blocks_per_tile = min(4, seq // w)
    t = blocks_per_tile * w
    assert seq % t == 0
    sq3 = sq.reshape(batch, seq, SWA_HEADS * dh)
    sk3 = sk.reshape(batch, seq, SWA_KV_HEADS * dh)
    sv3 = sv.reshape(batch, seq, SWA_KV_HEADS * dh)
    bucket = jnp.asarray(_t5_bucket_table())
    cur = lambda b, i: (b, i, 0)
    prev = lambda b, i: (b, jnp.maximum(i * blocks_per_tile - 1, 0), 0)
    smem = pl.BlockSpec(memory_space=pltpu.SMEM)
    out = pl.pallas_call(
        functools.partial(_swa_kernel, blocks_per_tile=blocks_per_tile),
        grid=(batch, seq // t),
        in_specs=[
            smem, smem,
            pl.BlockSpec((w, 2 * w), lambda b, i: (0, 0)),
            pl.BlockSpec((None, t, SWA_HEADS * dh), cur),
            pl.BlockSpec((None, t, SWA_KV_HEADS * dh), cur),
            pl.BlockSpec((None, t, SWA_KV_HEADS * dh), cur),
            pl.BlockSpec((None, w, SWA_KV_HEADS * dh), prev),
            pl.BlockSpec((None, w, SWA_KV_HEADS * dh), prev),
        ],
        out_specs=pl.BlockSpec((None, t, SWA_HEADS * dh), cur),
        out_shape=jax.ShapeDtypeStruct((batch, seq, SWA_HEADS * dh), BF16),
        scratch_shapes=[pltpu.VMEM((SWA_KV_HEADS, SWA_GROUP * w, 2 * w), F32)],
        compiler_params=pltpu.CompilerParams(dimension_semantics=("arbitrary", "arbitrary"),
                                             vmem_limit_bytes=VMEM_LIMIT_BYTES),
        name="swa",
    )(rel_bias, sinks, bucket, sq3, sk3, sv3, sk3, sv3)
    return out.reshape(batch * seq, SWA_HEADS * dh)


def _outproj_kernel(x_ref, fa_ref, sb_ref, wa_ref, wb_ref, g_ref, wrh_ref, wrl_ref, br_ref, lstrict_ref,
                    h_ref, xs_ref, meta_ref, gates_ref, counts_ref, carry_ref, *, n_experts):
    i = pl.program_id(0)
    tm = x_ref.shape[0]

    @pl.when(i == 0)
    def _():
        carry_ref[...] = jnp.zeros_like(carry_ref)

    h = (x_ref[...] + jnp.dot(fa_ref[...], wa_ref[...], preferred_element_type=F32)
         + jnp.dot(sb_ref[...], wb_ref[...], preferred_element_type=F32))
    h_ref[...] = h
    ms = jnp.mean(h * h, axis=-1, keepdims=True)
    hn = h * lax.rsqrt(ms + EPS) * g_ref[...]
    xs_ref[...] = _pack_bf16_pairs(hn)

    hn_hi = hn.astype(BF16)
    hn_lo = (hn - hn_hi.astype(F32)).astype(BF16)
    wrh = wrh_ref[...]
    logits = (jnp.dot(hn_hi, wrh, preferred_element_type=F32) + jnp.dot(hn_lo, wrh, preferred_element_type=F32)
              + jnp.dot(hn_hi, wrl_ref[...], preferred_element_type=F32) + br_ref[...])
    lane = _lane_iota((tm, LANES))
    lane_f = lane.astype(F32)
    l = jnp.where(lane < n_experts, logits, NEG_INF)
    vals, idxs = [], []
    member = jnp.zeros((tm, LANES), F32)
    for _ in range(TOP_K):
        mk = jnp.max(l, axis=-1, keepdims=True)
        ik = jnp.min(jnp.where(l == mk, lane_f, float(LANES)), axis=-1, keepdims=True)
        sel = lane_f == ik
        l = jnp.where(sel, NEG_INF, l)
        member = jnp.where(sel, 1.0, member)
        vals.append(mk)
        idxs.append(ik)
    es = [jnp.exp(v - vals[0]) for v in vals]
    tot = es[0] + es[1] + es[2] + es[3]
    gates = [e / tot for e in es]

    rank = jnp.dot(lstrict_ref[...], member.astype(BF16), preferred_element_type=F32) + carry_ref[0:1, :]
    carry_new = rank[tm - 1:tm, :] + member[tm - 1:tm, :]
    carry_ref[0:1, :] = carry_new
    counts_ref[...] = jnp.broadcast_to(carry_new, counts_ref.shape)
    ranks = [jnp.sum(jnp.where(lane_f == ik, rank, 0.0), axis=-1, keepdims=True) for ik in idxs]

    meta = jnp.zeros((tm, LANES), F32)
    gt = jnp.zeros((tm, LANES), F32)
    for k in range(TOP_K):
        meta = jnp.where(lane == k, idxs[k], meta)
        meta = jnp.where(lane == TOP_K + k, ranks[k], meta)
        gt = jnp.where(lane == k, gates[k], gt)
    meta_ref[...] = meta.astype(jnp.int32)
    gates_ref[...] = gt


def _outproj(x2d, fa, sb, w_out, norm_moe, w_router, b_router):
    n, d = x2d.shape
    n_experts = w_router.shape[1]
    assert n_experts <= LANES
    tm = min(512, n)
    assert n % tm == 0
    nfa = FOX_HEADS * HEAD_DIM
    wa = w_out[:nfa].astype(BF16)
    wb = w_out[nfa:].reshape(SWA_KV_HEADS, SWA_GROUP, HEAD_DIM, d).transpose(1, 0, 2, 3).reshape(SWA_HEADS * HEAD_DIM, d)
    wb = wb.astype(BF16)
    wr = jnp.zeros((d, LANES), F32).at[:, :n_experts].set(w_router)
    wr_hi = wr.astype(BF16)
    wr_lo = (wr - wr_hi.astype(F32)).astype(BF16)
    br = jnp.zeros((1, LANES), F32).at[0, :n_experts].set(b_router)
    lstrict = jnp.asarray(np.tril(np.ones((tm, tm), np.float32), -1), BF16)
    const = lambda i: (0, 0)
    row = lambda i: (i, 0)
    return pl.pallas_call(
        functools.partial(_outproj_kernel, n_experts=n_experts),
        grid=(n // tm,),
        in_specs=[
            pl.BlockSpec((tm, d), row),
            pl.BlockSpec((tm, nfa), row),
            pl.BlockSpec((tm, SWA_HEADS * HEAD_DIM), row),
            pl.BlockSpec((nfa, d), const),
            pl.BlockSpec((SWA_HEADS * HEAD_DIM, d), const),
            pl.BlockSpec((1, d), const),
            pl.BlockSpec((d, LANES), const),
            pl.BlockSpec((d, LANES), const),
            pl.BlockSpec((1, LANES), const),
            pl.BlockSpec((tm, tm), const),
        ],
        out_specs=[
            pl.BlockSpec((tm, d), row),
            pl.BlockSpec((tm, d // 2), row),
            pl.BlockSpec((tm, LANES), row),
            pl.BlockSpec((tm, LANES), row),
            pl.BlockSpec((8, LANES), const),
        ],
        out_shape=(
            jax.ShapeDtypeStruct((n, d), F32),
            jax.ShapeDtypeStruct((n, d // 2), jnp.uint32),
            jax.ShapeDtypeStruct((n, LANES), jnp.int32),
            jax.ShapeDtypeStruct((n, LANES), F32),
            jax.ShapeDtypeStruct((8, LANES), F32),
        ),
        scratch_shapes=[pltpu.VMEM((8, LANES), F32)],
        compiler_params=pltpu.CompilerParams(dimension_semantics=("arbitrary",), vmem_limit_bytes=VMEM_LIMIT_BYTES),
        name="outproj",
    )(x2d, fa, sb, wa, wb, norm_moe.reshape(1, d), wr_hi, wr_lo, br, lstrict)


DMA_ISSUE_UNROLL = 8


def _dispatch_kernel(pos_ref, x_ref, init_ref, out_ref, sem):
    del init_ref
    tm = x_ref.shape[0]

    def row_copy(t, p):
        return pltpu.make_async_copy(x_ref.at[pl.ds(t, 1), :], out_ref.at[pl.ds(p, 1), :], sem)

    def start(t, c):
        for k in range(TOP_K):
            row_copy(t, pos_ref[0, 0, t * TOP_K + k]).start(priority=k % 2)
        return c

    lax.fori_loop(0, tm, start, 0, unroll=DMA_ISSUE_UNROLL)
    for k in range(TOP_K):
        pltpu.make_async_copy(x_ref, out_ref.at[pl.ds(0, tm), :], sem).wait()


def _dispatch(xs_tok, pos, rows_total):
    n, w = xs_tok.shape
    tm = min(512, n)
    pos3 = pos.reshape(n // tm, 1, tm * TOP_K)
    init = jnp.zeros((rows_total, w), xs_tok.dtype)
    return pl.pallas_call(
        _dispatch_kernel,
        grid=(n // tm,),
        in_specs=[
            pl.BlockSpec((1, 1, tm * TOP_K), lambda i: (i, 0, 0), memory_space=pltpu.SMEM),
            pl.BlockSpec((tm, w), lambda i: (i, 0)),
            pl.BlockSpec(memory_space=pl.ANY),
        ],
        out_specs=pl.BlockSpec(memory_space=pl.ANY),
        out_shape=jax.ShapeDtypeStruct((rows_total, w), xs_tok.dtype),
        scratch_shapes=[pltpu.SemaphoreType.DMA(())],
        input_output_aliases={2: 0},
        compiler_params=pltpu.CompilerParams(dimension_semantics=("arbitrary",), vmem_limit_bytes=VMEM_LIMIT_BYTES,
                                             has_side_effects=True),
        name="dispatch",
    )(pos3, xs_tok, init)


def _experts_kernel(be_ref, bv_ref, x_ref, wug_ref, bug_ref, wd_ref, bd_ref, y_ref, wug_bf, wd_bf):
    i = pl.program_id(0)
    f = wd_ref.shape[0]
    prev = be_ref[jnp.maximum(i - 1, 0)]

    @pl.when((i == 0) | (be_ref[i] != prev))
    def _():
        wug_bf[...] = wug_ref[...].astype(BF16)
        wd_bf[...] = wd_ref[...].astype(BF16)

    @pl.when(bv_ref[i] > 0)
    def _():
        first, second = _unpack_bf16_pairs(x_ref[...])
        x = jnp.concatenate([first.astype(BF16), second.astype(BF16)], axis=1)
        gu = jnp.dot(x, wug_bf[...], preferred_element_type=F32) + bug_ref[...]
        gate = jnp.minimum(gu[:, :f], SWIGLU_LIMIT)
        up = jnp.clip(gu[:, f:], -SWIGLU_LIMIT, SWIGLU_LIMIT)
        act = (up + 1.0) * (gate * (1.0 / (1.0 + jnp.exp(-SWIGLU_ALPHA * gate))))
        y = jnp.dot(act.astype(BF16), wd_bf[...], preferred_element_type=F32) + bd_ref[...]
        y_ref[...] = _pack_bf16_pairs(y)

    @pl.when(bv_ref[i] <= 0)
    def _():
        y_ref[...] = jnp.zeros_like(y_ref)


def _experts(xs_sorted, blk_exp, blk_valid, w_up_gate, b_up_gate, w_down, b_down, tr):
    rows_total, w = xs_sorted.shape
    e, d, f2 = w_up_gate.shape
    f = f2 // 2
    grid_spec = pltpu.PrefetchScalarGridSpec(
        num_scalar_prefetch=2,
        grid=(rows_total // tr,),
        in_specs=[
            pl.BlockSpec((tr, w), lambda i, be, bv: (i, 0)),
            pl.BlockSpec((None, d, f2), lambda i, be, bv: (be[i], 0, 0)),
            pl.BlockSpec((None, 1, f2), lambda i, be, bv: (be[i], 0, 0)),
            pl.BlockSpec((None, f, d), lambda i, be, bv: (be[i], 0, 0)),
            pl.BlockSpec((None, 1, d), lambda i, be, bv: (be[i], 0, 0)),
        ],
        out_specs=pl.BlockSpec((tr, d // 2), lambda i, be, bv: (i, 0)),
        scratch_shapes=[pltpu.VMEM((d, f2), BF16), pltpu.VMEM((f, d), BF16)],
    )
    return pl.pallas_call(
        _experts_kernel,
        grid_spec=grid_spec,
        out_shape=jax.ShapeDtypeStruct((rows_total, d // 2), jnp.uint32),
        compiler_params=pltpu.CompilerParams(dimension_semantics=("arbitrary",), vmem_limit_bytes=VMEM_LIMIT_BYTES),
        name="experts",
    )(blk_exp, blk_valid, xs_sorted, w_up_gate, b_up_gate.reshape(e, 1, f2), w_down, b_down.reshape(e, 1, d))


def _combine_kernel(pos_ref, posn_ref, h_ref, gates_ref, ys_ref, o_ref, buf, sems):
    i = pl.program_id(0)
    n_steps = pl.num_programs(0)
    tm = h_ref.shape[0]
    slot = i % 2

    def row_copy(p_ref, t, k, s):
        p = p_ref[0, 0, t * TOP_K + k]
        return pltpu.make_async_copy(ys_ref.at[pl.ds(p, 1), :], buf.at[s, k, pl.ds(t, 1), :], sems.at[s, k])

    def issue(p_ref, s):
        def body(t, c):
            for k in range(TOP_K):
                row_copy(p_ref, t, k, s).start(priority=k % 2)
            return c
        lax.fori_loop(0, tm, body, 0, unroll=DMA_ISSUE_UNROLL)

    @pl.when(i == 0)
    def _():
        issue(pos_ref, 0)

    @pl.when(i + 1 < n_steps)
    def _():
        issue(posn_ref, 1 - slot)

    for k in range(TOP_K):
        pltpu.make_async_copy(ys_ref.at[pl.ds(0, tm), :], buf.at[slot, k], sems.at[slot, k]).wait()

    acc = h_ref[...]
    gates = gates_ref[...]
    for k in range(TOP_K):
        first, second = _unpack_bf16_pairs(buf[slot, k])
        acc = acc + gates[:, k:k + 1] * jnp.concatenate([first, second], axis=1)
    o_ref[...] = acc


def _combine(h, gates, pos, ys):
    n, d = h.shape
    tm = min(256, n)
    nt = n // tm
    pos3 = pos.reshape(nt, 1, tm * TOP_K)
    smem_blk = lambda imap: pl.BlockSpec((1, 1, tm * TOP_K), imap, memory_space=pltpu.SMEM)
    return pl.pallas_call(
        _combine_kernel,
        grid=(nt,),
        in_specs=[
            smem_blk(lambda i: (i, 0, 0)),
            smem_blk(lambda i: (jnp.minimum(i + 1, nt - 1), 0, 0)),
            pl.BlockSpec((tm, d), lambda i: (i, 0)),
            pl.BlockSpec((tm, LANES), lambda i: (i, 0)),
            pl.BlockSpec(memory_space=pl.ANY),
        ],
        out_specs=pl.BlockSpec((tm, d), lambda i: (i, 0)),
        out_shape=jax.ShapeDtypeStruct((n, d), F32),
        scratch_shapes=[pltpu.VMEM((2, TOP_K, tm, d // 2), jnp.uint32), pltpu.SemaphoreType.DMA((2, TOP_K))],
        compiler_params=pltpu.CompilerParams(dimension_semantics=("arbitrary",), vmem_limit_bytes=VMEM_LIMIT_BYTES),
        name="combine",
    )(pos3, pos3, h, gates, ys)


EXPERT_TILE = 256


def _layer(h2d, batch, seq, norm_mix, w_in, b_forget, q_norm_fox, k_norm_fox, q_norm_swa, k_norm_swa, sinks, rel_bias,
           w_out, norm_moe, w_router, b_router, w_up_gate, b_up_gate, w_down, b_down):
    n, d = h2d.shape
    n_experts = w_router.shape[1]
    qa, ka, fv, sq, sk, sv = _proj(h2d, norm_mix, w_in, b_forget, q_norm_fox, k_norm_fox, q_norm_swa, k_norm_swa, seq)
    fa = _fox(qa, ka, fv, batch, seq)
    sb = _swa(sq, sk, sv, sinks, rel_bias, batch, seq)
    h, xs_tok, meta, gates, counts = _outproj(h2d, fa, sb, w_out, norm_moe, w_router, b_router)

    tr = min(EXPERT_TILE, n)
    counts_i = counts[0, :n_experts].astype(jnp.int32)
    padded = (counts_i + tr - 1) // tr * tr
    pend = jnp.cumsum(padded)
    pstart = pend - padded
    idx = meta[:, :TOP_K]
    rank = meta[:, TOP_K:2 * TOP_K]
    pos = pstart[idx] + rank
    rows_total = n * TOP_K + n_experts * tr
    n_tiles = rows_total // tr
    blk_start = jnp.arange(n_tiles, dtype=jnp.int32) * tr
    blk_exp = jnp.minimum(jnp.sum(pend[None, :] <= blk_start[:, None], axis=1), n_experts - 1).astype(jnp.int32)
    blk_valid = jnp.clip(pstart[blk_exp] + counts_i[blk_exp] - blk_start, 0, tr).astype(jnp.int32)

    xs_sorted = _dispatch(xs_tok, pos.astype(jnp.int32), rows_total)
    ys = _experts(xs_sorted, blk_exp, blk_valid, w_up_gate, b_up_gate, w_down, b_down, tr)
    return _combine(h, gates, pos.astype(jnp.int32), ys)


def kernel(x, norm_mix, w_in, b_forget, q_norm_fox, k_norm_fox, q_norm_swa, k_norm_swa, sinks, rel_bias, w_out, norm_moe,
           w_router, b_router, w_up_gate, b_up_gate, w_down, b_down):
    batch, seq, d = x.shape
    h = x.reshape(batch * seq, d)
    for l in range(norm_mix.shape[0]):
        h = _layer(h, batch, seq, norm_mix[l], w_in[l], b_forget[l], q_norm_fox[l], k_norm_fox[l], q_norm_swa[l],
                   k_norm_swa[l], sinks[l], rel_bias, w_out[l], norm_moe[l], w_router[l], b_router[l], w_up_gate[l],
                   b_up_gate[l], w_down[l], b_down[l])
    return h.reshape(batch, seq, d)
```

```python
import functools
import math

import numpy as np
import jax
import jax.numpy as jnp
from jax import lax
from jax.experimental import pallas as pl
from jax.experimental.pallas import tpu as pltpu

HEAD_DIM = 64
FOX_HEADS = 8
SWA_HEADS = 8
SWA_KV_HEADS = 2
SWA_GROUP = SWA_HEADS // SWA_KV_HEADS
WINDOW = 128
N_BUCKETS = 32
MAX_DISTANCE = 128
TOP_K = 4
SWIGLU_LIMIT = 7.0
SWIGLU_ALPHA = 1.702
EPS = 1e-6

LANES = 128
VMEM_LIMIT_BYTES = 56 * 1024 * 1024
NEG_INF = float("-inf")
LOG2E = math.log2(math.e)

F32 = jnp.float32
BF16 = jnp.bfloat16


def _lane_iota(shape):
    return lax.broadcasted_iota(jnp.int32, shape, len(shape) - 1)


def _split3_bf16(v):
    hi = v.astype(BF16)
    r1 = v - hi.astype(F32)
    mid = r1.astype(BF16)
    lo = (r1 - mid.astype(F32)).astype(BF16)
    return hi, mid, lo


def _pack_bf16_pairs(v):
    w = v.shape[1] // 2
    u = pltpu.bitcast(v.astype(BF16).astype(F32), jnp.uint32)
    return u[:, :w] | (u[:, w:] >> 16)


def _unpack_bf16_pairs(u):
    first = pltpu.bitcast(u & jnp.uint32(0xFFFF0000), F32)
    second = pltpu.bitcast(u << 16, F32)
    return first, second


CUM_CHUNK = 128


def _proj_kernel(x_ref, g_ref, wqa_ref, wka_ref, wrest_ref, bf_ref, gqa_ref, gka_ref, gsq_ref, gsk_ref, tri_ref,
                 qa_ref, ka_ref, fv_ref, sq_ref, sk_ref, sv_ref, carry_ref, *, tiles_per_seq):
    i = pl.program_id(0)
    tm = x_ref.shape[0]
    nh = FOX_HEADS
    scale = HEAD_DIM ** -0.5

    @pl.when(i % tiles_per_seq == 0)
    def _():
        carry_ref[...] = jnp.zeros_like(carry_ref)

    x = x_ref[...]
    ms = jnp.mean(x * x, axis=-1, keepdims=True)
    hn = (x * lax.rsqrt(ms + EPS) * g_ref[...]).astype(BF16)

    qa = jnp.dot(hn, wqa_ref[...], preferred_element_type=F32)
    ka = jnp.dot(hn, wka_ref[...], preferred_element_type=F32)
    rest = jnp.dot(hn, wrest_ref[...], preferred_element_type=F32)

    lane_w = _lane_iota((tm, nh * LANES)) % LANES
    upper = lane_w >= HEAD_DIM

    fl = qa + bf_ref[...]
    ls = jnp.minimum(fl, 0.0) - jnp.log(1.0 + jnp.exp(-jnp.abs(fl)))
    ls = jnp.where(upper, ls, 0.0)
    tri = tri_ref[...]
    carry = carry_ref[0:1, :]
    c_chunks = []
    for r in range(tm // CUM_CHUNK):
        hi, mid, lo = _split3_bf16(ls[r * CUM_CHUNK:(r + 1) * CUM_CHUNK, :])
        cs = jnp.dot(tri, jnp.concatenate([hi, mid, lo], axis=0), preferred_element_type=F32) + carry
        carry = cs[CUM_CHUNK - 1:CUM_CHUNK, :]
        c_chunks.append(cs)
    carry_ref[0:1, :] = carry
    c = jnp.concatenate(c_chunks, axis=0) * LOG2E

    c1 = c.astype(BF16).astype(F32)
    c2 = (c - c1).astype(BF16).astype(F32)
    c3 = c - c1 - c2

    def head_norm(t, g_row):
        outs = []
        for h in range(t.shape[1] // LANES):
            th = t[:, h * LANES:(h + 1) * LANES]
            lo_mask = _lane_iota(th.shape) < HEAD_DIM
            ss = jnp.sum(jnp.where(lo_mask, th * th, 0.0), axis=-1, keepdims=True)
            outs.append(th * lax.rsqrt(ss * (1.0 / HEAD_DIM) + EPS))
        return jnp.concatenate(outs, axis=1) * g_row

    qn = head_norm(qa, gqa_ref[...]) * (scale * LOG2E)
    kn = head_norm(ka, gka_ref[...])

    one = jnp.float32(1.0)
    zero = jnp.float32(0.0)
    q_aug = jnp.where(lane_w == 64, c1, jnp.where(lane_w == 65, c2, jnp.where(lane_w == 66, c3,
                      jnp.where(lane_w < 70, one, zero))))
    k_aug = jnp.where(lane_w < 67, one, jnp.where(lane_w == 67, -c1, jnp.where(lane_w == 68, -c2,
                      jnp.where(lane_w == 69, -c3, zero))))
    qa_ref[...] = jnp.where(upper, q_aug, qn).astype(BF16)
    ka_ref[...] = jnp.where(upper, k_aug, kn).astype(BF16)

    nfv = FOX_HEADS * HEAD_DIM
    nsq = SWA_HEADS * HEAD_DIM
    nsk = SWA_KV_HEADS * HEAD_DIM
    fv_ref[...] = rest[:, :nfv].astype(BF16)

    def half_norm(t, g_row):
        outs = []
        for j in range(t.shape[1] // LANES):
            tj = t[:, j * LANES:(j + 1) * LANES]
            lo_mask = _lane_iota(tj.shape) < HEAD_DIM
            sq = tj * tj
            s_lo = jnp.sum(jnp.where(lo_mask, sq, 0.0), axis=-1, keepdims=True)
            s_hi = jnp.sum(jnp.where(lo_mask, 0.0, sq), axis=-1, keepdims=True)
            r = jnp.where(lo_mask, lax.rsqrt(s_lo * (1.0 / HEAD_DIM) + EPS), lax.rsqrt(s_hi * (1.0 / HEAD_DIM) + EPS))
            outs.append(tj * r)
        return jnp.concatenate(outs, axis=1) * g_row

    sq_ref[...] = (half_norm(rest[:, nfv:nfv + nsq], gsq_ref[...]) * scale).astype(BF16)
    sk_ref[...] = half_norm(rest[:, nfv + nsq:nfv + nsq + nsk], gsk_ref[...]).astype(BF16)
    sv_ref[...] = rest[:, nfv + nsq + nsk:].astype(BF16)


def _proj(x2d, norm_mix, w_in, b_forget, q_norm_fox, k_norm_fox, q_norm_swa, k_norm_swa, seq):
    n, d = x2d.shape
    nh, dh = FOX_HEADS, HEAD_DIM
    tm = min(512, seq)
    assert seq % tm == 0 and tm % CUM_CHUNK == 0
    points = np.cumsum([nh * dh, nh * dh, nh * dh, nh, SWA_HEADS * dh, SWA_KV_HEADS * dh, SWA_KV_HEADS * dh])
    wfq, wfk, wfv, wff, wsq, wsk, wsv = jnp.split(w_in, [int(p) for p in points[:-1]], axis=1)

    wfq = wfq.reshape(d, nh, dh)
    wff_rep = jnp.broadcast_to(wff[:, :, None], (d, nh, dh))
    wqa = jnp.concatenate([wfq, wff_rep], axis=2).reshape(d, nh * LANES).astype(BF16)
    wka = jnp.concatenate([wfk.reshape(d, nh, dh), jnp.zeros((d, nh, dh), w_in.dtype)], axis=2)
    wka = wka.reshape(d, nh * LANES).astype(BF16)
    wsq = wsq.reshape(d, SWA_KV_HEADS, SWA_GROUP, dh).transpose(0, 2, 1, 3).reshape(d, SWA_HEADS * dh)
    wrest = jnp.concatenate([wfv, wsq, wsk, wsv], axis=1).astype(BF16)

    zeros_h = jnp.zeros((nh, dh), F32)
    bf_row = jnp.concatenate([zeros_h, jnp.broadcast_to(b_forget[:, None], (nh, dh))], axis=1).reshape(1, nh * LANES)
    gqa = jnp.concatenate([jnp.broadcast_to(q_norm_fox[None], (nh, dh)), zeros_h], axis=1).reshape(1, nh * LANES)
    gka = jnp.concatenate([jnp.broadcast_to(k_norm_fox[None], (nh, dh)), zeros_h], axis=1).reshape(1, nh * LANES)
    gsq = jnp.tile(q_norm_swa, SWA_HEADS).reshape(1, SWA_HEADS * dh)
    gsk = jnp.tile(k_norm_swa, SWA_KV_HEADS).reshape(1, SWA_KV_HEADS * dh)
    tri = np.tril(np.ones((CUM_CHUNK, CUM_CHUNK), np.float32))
    tri3 = jnp.asarray(np.concatenate([tri, tri, tri], axis=1), BF16)

    nrest = wrest.shape[1]
    const = lambda i: (0, 0)
    row = lambda i: (i, 0)
    out_shapes = (
        jax.ShapeDtypeStruct((n, nh * LANES), BF16),
        jax.ShapeDtypeStruct((n, nh * LANES), BF16),
        jax.ShapeDtypeStruct((n, nh * dh), BF16),
        jax.ShapeDtypeStruct((n, SWA_HEADS * dh), BF16),
        jax.ShapeDtypeStruct((n, SWA_KV_HEADS * dh), BF16),
        jax.ShapeDtypeStruct((n, SWA_KV_HEADS * dh), BF16),
    )
    return pl.pallas_call(
        functools.partial(_proj_kernel, tiles_per_seq=seq // tm),
        grid=(n // tm,),
        in_specs=[
            pl.BlockSpec((tm, d), row),
            pl.BlockSpec((1, d), const),
            pl.BlockSpec((d, nh * LANES), const),
            pl.BlockSpec((d, nh * LANES), const),
            pl.BlockSpec((d, nrest), const),
            pl.BlockSpec((1, nh * LANES), const),
            pl.BlockSpec((1, nh * LANES), const),
            pl.BlockSpec((1, nh * LANES), const),
            pl.BlockSpec((1, SWA_HEADS * dh), const),
            pl.BlockSpec((1, SWA_KV_HEADS * dh), const),
            pl.BlockSpec((CUM_CHUNK, 3 * CUM_CHUNK), const),
        ],
        out_specs=[
            pl.BlockSpec((tm, nh * LANES), row),
            pl.BlockSpec((tm, nh * LANES), row),
            pl.BlockSpec((tm, nh * dh), row),
            pl.BlockSpec((tm, SWA_HEADS * dh), row),
            pl.BlockSpec((tm, SWA_KV_HEADS * dh), row),
            pl.BlockSpec((tm, SWA_KV_HEADS * dh), row),
        ],
        out_shape=out_shapes,
        scratch_shapes=[pltpu.VMEM((8, nh * LANES), F32)],
        compiler_params=pltpu.CompilerParams(dimension_semantics=("arbitrary",), vmem_limit_bytes=VMEM_LIMIT_BYTES),
        name="proj",
    )(x2d, norm_mix.reshape(1, d), wqa, wka, wrest, bf_row, gqa, gka, gsq, gsk, tri3)


FOX_BLOCK = 512


def _fox_kernel(qa_ref, ka_ref, v_ref, o_ref, s0_sc, s1_sc, m_sc, l_sc, acc_sc, *, blk):
    i = pl.program_id(2)
    nt = (((1,), (1,)), ((), ()))
    row = lax.broadcasted_iota(jnp.int32, (blk, blk), 0)
    col = lax.broadcasted_iota(jnp.int32, (blk, blk), 1)
    causal = col <= row
    s_bufs = (s0_sc, s1_sc)

    def lane_tiles(s):
        return [s[:, c * LANES:(c + 1) * LANES] for c in range(s.shape[1] // LANES)]

    def row_max(tiles):
        m = jnp.max(functools.reduce(jnp.maximum, tiles), axis=-1, keepdims=True)
        return jnp.broadcast_to(m, (blk, LANES))

    def row_sum(tiles):
        t = jnp.sum(functools.reduce(jnp.add, tiles), axis=-1, keepdims=True)
        return jnp.broadcast_to(t, (blk, LANES))

    def produce_scores(j, slot):
        start = pl.multiple_of(j * blk, blk)
        for hh in range(2):
            q = qa_ref[:, hh * LANES:(hh + 1) * LANES]
            k = ka_ref[pl.ds(start, blk), hh * LANES:(hh + 1) * LANES]
            s_bufs[slot][hh] = lax.dot_general(q, k, nt, preferred_element_type=F32)

    def consume_scores(j, slot, masked):
        v = v_ref[pl.ds(pl.multiple_of(j * blk, blk), blk), :]
        for hh in range(2):
            s = s_bufs[slot][hh]
            if masked:
                s = jnp.where(causal, s, NEG_INF)
            tiles = lane_tiles(s)
            m = m_sc[hh]
            m_new = jnp.maximum(m, row_max(tiles))
            alpha = jnp.exp2(m - m_new)
            p_tiles = [jnp.exp2(t - m_new) for t in tiles]
            p = jnp.concatenate(p_tiles, axis=1).astype(BF16)
            m_sc[hh] = m_new
            l_sc[hh] = alpha * l_sc[hh] + row_sum(p_tiles)
            acc_sc[hh] = alpha * acc_sc[hh] + jnp.dot(p, v, preferred_element_type=F32)

    m_sc[...] = jnp.full(m_sc.shape, NEG_INF, F32)
    l_sc[...] = jnp.zeros(l_sc.shape, F32)
    acc_sc[...] = jnp.zeros(acc_sc.shape, F32)
    produce_scores(0, 0)

    def pair(jj, c):
        j = 2 * jj
        produce_scores(j + 1, 1)
        consume_scores(j, 0, masked=False)
        produce_scores(j + 2, 0)
        consume_scores(j + 1, 1, masked=False)
        return c

    lax.fori_loop(0, i // 2, pair, 0)

    @pl.when(i % 2 == 0)
    def _():
        consume_scores(i, 0, masked=True)

    @pl.when(i % 2 == 1)
    def _():
        produce_scores(i, 1)
        consume_scores(i - 1, 0, masked=False)
        consume_scores(i, 1, masked=True)

    out0 = acc_sc[0] / l_sc[0]
    out1 = acc_sc[1] / l_sc[1]
    lane = _lane_iota(out0.shape)
    o_ref[...] = jnp.where(lane < HEAD_DIM, out0, out1).astype(o_ref.dtype)


def _fox(qa, ka, fv, batch, seq):
    nh, dh = FOX_HEADS, HEAD_DIM
    blk = min(FOX_BLOCK, seq)
    assert seq % blk == 0
    qa3 = qa.reshape(batch, seq, nh * LANES)
    ka3 = ka.reshape(batch, seq, nh * LANES)
    fv3 = fv.reshape(batch, seq, nh * dh)
    out = pl.pallas_call(
        functools.partial(_fox_kernel, blk=blk),
        grid=(batch, nh // 2, seq // blk),
        in_specs=[
            pl.BlockSpec((None, blk, 2 * LANES), lambda b, j, i: (b, i, j)),
            pl.BlockSpec((None, seq, 2 * LANES), lambda b, j, i: (b, 0, j)),
            pl.BlockSpec((None, seq, LANES), lambda b, j, i: (b, 0, j)),
        ],
        out_specs=pl.BlockSpec((None, blk, LANES), lambda b, j, i: (b, i, j)),
        out_shape=jax.ShapeDtypeStruct((batch, seq, nh * dh), BF16),
        scratch_shapes=[
            pltpu.VMEM((2, blk, blk), F32),
            pltpu.VMEM((2, blk, blk), F32),
            pltpu.VMEM((2, blk, LANES), F32),
            pltpu.VMEM((2, blk, LANES), F32),
            pltpu.VMEM((2, blk, LANES), F32),
        ],
        compiler_params=pltpu.CompilerParams(dimension_semantics=("arbitrary", "arbitrary", "arbitrary"),
                                             vmem_limit_bytes=VMEM_LIMIT_BYTES),
        name="fox",
    )(qa3, ka3, fv3)
    return out.reshape(batch * seq, nh * dh)


def _t5_bucket_table():
    qi = np.arange(WINDOW)[:, None]
    kj = np.arange(2 * WINDOW)[None, :]
    dist = qi + WINDOW - kj
    n = np.maximum(dist, 0)
    max_exact = N_BUCKETS // 2
    nf = np.maximum(n, 1).astype(np.float64)
    large = max_exact + (np.log(nf / max_exact) / math.log(MAX_DISTANCE / max_exact) * (N_BUCKETS - max_exact)).astype(np.int64)
    large = np.minimum(large, N_BUCKETS - 1)
    bucket = np.where(n < max_exact, n, large)
    valid = (dist >= 0) & (dist < WINDOW)
    return np.where(valid, bucket, -1).astype(np.int32)


def _swa_kernel(rb_ref, sinks_ref, bucket_ref, sq_ref, sk_ref, sv_ref, skp_ref, svp_ref, o_ref, bias_ref, *, blocks_per_tile):
    b = pl.program_id(0)
    i = pl.program_id(1)
    w = WINDOW
    g = SWA_GROUP

    @pl.when((b == 0) & (i == 0))
    def _():
        bucket = bucket_ref[...]
        for h in range(SWA_HEADS):
            acc = jnp.full((w, 2 * w), NEG_INF, F32)
            for bk in range(N_BUCKETS):
                acc = jnp.where(bucket == bk, rb_ref[bk, h], acc)
            kv, gi = divmod(h, g)
            bias_ref[kv, gi * w:(gi + 1) * w, :] = acc

    nt = (((1,), (1,)), ((), ()))
    lane = _lane_iota((w, LANES))
    col = _lane_iota((g * w, 2 * w))
    for cb in range(blocks_per_tile):
        rows = slice(cb * w, (cb + 1) * w)
        if cb == 0:
            kprev, vprev = skp_ref[...], svp_ref[...]
        else:
            kprev, vprev = sk_ref[(cb - 1) * w:cb * w, :], sv_ref[(cb - 1) * w:cb * w, :]
        kcat = jnp.concatenate([kprev, sk_ref[rows, :]], axis=0)
        vcat = jnp.concatenate([vprev, sv_ref[rows, :]], axis=0)
        outs = []
        for kv in range(SWA_KV_HEADS):
            keep = (lane < HEAD_DIM) if kv == 0 else (lane >= HEAD_DIM)
            qs = [jnp.where(keep, sq_ref[rows, gi * LANES:(gi + 1) * LANES], jnp.zeros((), BF16)) for gi in range(g)]
            qstack = jnp.concatenate(qs, axis=0)
            s = lax.dot_general(qstack, kcat, nt, preferred_element_type=F32) + bias_ref[kv]
            if cb == 0:
                s = s + jnp.where(col < w, jnp.where(i == 0, NEG_INF, 0.0), 0.0)
            sink = jnp.concatenate([jnp.full((w, 1), sinks_ref[kv * g + gi], F32) for gi in range(g)], axis=0)
            m = jnp.maximum(jnp.max(s, axis=-1, keepdims=True), sink)
            p = jnp.exp(s - m)
            denom = jnp.sum(p, axis=-1, keepdims=True) + jnp.exp(sink - m)
            outs.append(jnp.dot(p.astype(BF16), vcat, preferred_element_type=F32) / denom)
        for gi in range(g):
            o_ref[rows, gi * LANES:(gi + 1) * LANES] = jnp.where(
                lane < HEAD_DIM, outs[0][gi * w:(gi + 1) * w, :], outs[1][gi * w:(gi + 1) * w, :]).astype(o_ref.dtype)


def _swa(sq, sk, sv, sinks, rel_bias, batch, seq):
    w = WINDOW
    dh = HEAD_DIM
    blocks_per_tile = min(4, seq // w)
    t = blocks_per_tile * w
    assert seq % t == 0
    sq3 = sq.reshape(batch, seq, SWA_HEADS * dh)
    sk3 = sk.reshape(batch, seq, SWA_KV_HEADS * dh)
    sv3 = sv.reshape(batch, seq, SWA_KV_HEADS * dh)
    bucket = jnp.asarray(_t5_bucket_table())
    cur = lambda b, i: (b, i, 0)
    prev = lambda b, i: (b, jnp.maximum(i * blocks_per_tile - 1, 0), 0)
    smem = pl.BlockSpec(memory_space=pltpu.SMEM)
    out = pl.pallas_call(
        functools.partial(_swa_kernel, blocks_per_tile=blocks_per_tile),
        grid=(batch, seq // t),
        in_specs=[
            smem, smem,
            pl.BlockSpec((w, 2 * w), lambda b, i: (0, 0)),
            pl.BlockSpec((None, t, SWA_HEADS * dh), cur),
            pl.BlockSpec((None, t, SWA_KV_HEADS * dh), cur),
            pl.BlockSpec((None, t, SWA_KV_HEADS * dh), cur),
            pl.BlockSpec((None, w, SWA_KV_HEADS * dh), prev),
            pl.BlockSpec((None, w, SWA_KV_HEADS * dh), prev),
        ],
        out_specs=pl.BlockSpec((None, t, SWA_HEADS * dh), cur),
        out_shape=jax.ShapeDtypeStruct((batch, seq, SWA_HEADS * dh), BF16),
        scratch_shapes=[pltpu.VMEM((SWA_KV_HEADS, SWA_GROUP * w, 2 * w), F32)],
        compiler_params=pltpu.CompilerParams(dimension_semantics=("arbitrary", "arbitrary"),
                                             vmem_limit_bytes=VMEM_LIMIT_BYTES),
        name="swa",
    )(rel_bias, sinks, bucket, sq3, sk3, sv3, sk3, sv3)
    return out.reshape(batch * seq, SWA_HEADS * dh)


def _outproj_kernel(x_ref, fa_ref, sb_ref, wa_ref, wb_ref, g_ref, wrh_ref, wrl_ref, br_ref, lstrict_ref,
                    h_ref, xs_ref, meta_ref, gates_ref, counts_ref, carry_ref, *, n_experts):
    i = pl.program_id(0)
    tm = x_ref.shape[0]

    @pl.when(i == 0)
    def _():
        carry_ref[...] = jnp.zeros_like(carry_ref)

    h = (x_ref[...] + jnp.dot(fa_ref[...], wa_ref[...], preferred_element_type=F32)
         + jnp.dot(sb_ref[...], wb_ref[...], preferred_element_type=F32))
    h_ref[...] = h
    ms = jnp.mean(h * h, axis=-1, keepdims=True)
    hn = h * lax.rsqrt(ms + EPS) * g_ref[...]
    xs_ref[...] = _pack_bf16_pairs(hn)

    hn_hi = hn.astype(BF16)
    hn_lo = (hn - hn_hi.astype(F32)).astype(BF16)
    wrh = wrh_ref[...]
    logits = (jnp.dot(hn_hi, wrh, preferred_element_type=F32) + jnp.dot(hn_lo, wrh, preferred_element_type=F32)
              + jnp.dot(hn_hi, wrl_ref[...], preferred_element_type=F32) + br_ref[...])
    lane = _lane_iota((tm, LANES))
    lane_f = lane.astype(F32)
    l = jnp.where(lane < n_experts, logits, NEG_INF)
    vals, idxs = [], []
    member = jnp.zeros((tm, LANES), F32)
    for _ in range(TOP_K):
        mk = jnp.max(l, axis=-1, keepdims=True)
        ik = jnp.min(jnp.where(l == mk, lane_f, float(LANES)), axis=-1, keepdims=True)
        sel = lane_f == ik
        l = jnp.where(sel, NEG_INF, l)
        member = jnp.where(sel, 1.0, member)
        vals.append(mk)
        idxs.append(ik)
    es = [jnp.exp(v - vals[0]) for v in vals]
    tot = es[0] + es[1] + es[2] + es[3]
    gates = [e / tot for e in es]

    rank = jnp.dot(lstrict_ref[...], member.astype(BF16), preferred_element_type=F32) + carry_ref[0:1, :]
    carry_new = rank[tm - 1:tm, :] + member[tm - 1:tm, :]
    carry_ref[0:1, :] = carry_new
    counts_ref[...] = jnp.broadcast_to(carry_new, counts_ref.shape)
    ranks = [jnp.sum(jnp.where(lane_f == ik, rank, 0.0), axis=-1, keepdims=True) for ik in idxs]

    meta = jnp.zeros((tm, LANES), F32)
    gt = jnp.zeros((tm, LANES), F32)
    for k in range(TOP_K):
        meta = jnp.where(lane == k, idxs[k], meta)
        meta = jnp.where(lane == TOP_K + k, ranks[k], meta)
        gt = jnp.where(lane == k, gates[k], gt)
    meta_ref[...] = meta.astype(jnp.int32)
    gates_ref[...] = gt


def _outproj(x2d, fa, sb, w_out, norm_moe, w_router, b_router):
    n, d = x2d.shape
    n_experts = w_router.shape[1]
    assert n_experts <= LANES
    tm = min(512, n)
    assert n % tm == 0
    nfa = FOX_HEADS * HEAD_DIM
    wa = w_out[:nfa].astype(BF16)
    wb = w_out[nfa:].reshape(SWA_KV_HEADS, SWA_GROUP, HEAD_DIM, d).transpose(1, 0, 2, 3).reshape(SWA_HEADS * HEAD_DIM, d)
    wb = wb.astype(BF16)
    wr = jnp.zeros((d, LANES), F32).at[:, :n_experts].set(w_router)
    wr_hi = wr.astype(BF16)
    wr_lo = (wr - wr_hi.astype(F32)).astype(BF16)
    br = jnp.zeros((1, LANES), F32).at[0, :n_experts].set(b_router)
    lstrict = jnp.asarray(np.tril(np.ones((tm, tm), np.float32), -1), BF16)
    const = lambda i: (0, 0)
    row = lambda i: (i, 0)
    return pl.pallas_call(
        functools.partial(_outproj_kernel, n_experts=n_experts),
        grid=(n // tm,),
        in_specs=[
            pl.BlockSpec((tm, d), row),
            pl.BlockSpec((tm, nfa), row),
            pl.BlockSpec((tm, SWA_HEADS * HEAD_DIM), row),
            pl.BlockSpec((nfa, d), const),
            pl.BlockSpec((SWA_HEADS * HEAD_DIM, d), const),
            pl.BlockSpec((1, d), const),
            pl.BlockSpec((d, LANES), const),
            pl.BlockSpec((d, LANES), const),
            pl.BlockSpec((1, LANES), const),
            pl.BlockSpec((tm, tm), const),
        ],
        out_specs=[
            pl.BlockSpec((tm, d), row),
            pl.BlockSpec((tm, d // 2), row),
            pl.BlockSpec((tm, LANES), row),
            pl.BlockSpec((tm, LANES), row),
            pl.BlockSpec((8, LANES), const),
        ],
        out_shape=(
            jax.ShapeDtypeStruct((n, d), F32),
            jax.ShapeDtypeStruct((n, d // 2), jnp.uint32),
            jax.ShapeDtypeStruct((n, LANES), jnp.int32),
            jax.ShapeDtypeStruct((n, LANES), F32),
            jax.ShapeDtypeStruct((8, LANES), F32),
        ),
        scratch_shapes=[pltpu.VMEM((8, LANES), F32)],
        compiler_params=pltpu.CompilerParams(dimension_semantics=("arbitrary",), vmem_limit_bytes=VMEM_LIMIT_BYTES),
        name="outproj",
    )(x2d, fa, sb, wa, wb, norm_moe.reshape(1, d), wr_hi, wr_lo, br, lstrict)


DMA_ISSUE_UNROLL = 8


def _dispatch_kernel(pos_ref, x_ref, out_ref, sem):
    tm = x_ref.shape[0]

    def row_copy(t, p):
        return pltpu.make_async_copy(x_ref.at[pl.ds(t, 1), :], out_ref.at[pl.ds(p, 1), :], sem)

    def start(t, c):
        for k in range(TOP_K):
            row_copy(t, pos_ref[0, 0, t * TOP_K + k]).start(priority=k % 2)
        return c

    lax.fori_loop(0, tm, start, 0, unroll=DMA_ISSUE_UNROLL)
    for k in range(TOP_K):
        pltpu.make_async_copy(x_ref, out_ref.at[pl.ds(0, tm), :], sem).wait()


def _dispatch(xs_tok, pos, rows_total):
    n, w = xs_tok.shape
    tm = min(512, n)
    pos3 = pos.reshape(n // tm, 1, tm * TOP_K)
    return pl.pallas_call(
        _dispatch_kernel,
        grid=(n // tm,),
        in_specs=[
            pl.BlockSpec((1, 1, tm * TOP_K), lambda i: (i, 0, 0), memory_space=pltpu.SMEM),
            pl.BlockSpec((tm, w), lambda i: (i, 0)),
        ],
        out_specs=pl.BlockSpec(memory_space=pl.ANY),
        out_shape=jax.ShapeDtypeStruct((rows_total, w), xs_tok.dtype),
        scratch_shapes=[pltpu.SemaphoreType.DMA(())],
        compiler_params=pltpu.CompilerParams(dimension_semantics=("arbitrary",), vmem_limit_bytes=VMEM_LIMIT_BYTES,
                                             has_side_effects=True),
        name="dispatch",
    )(pos3, xs_tok)


def _experts_kernel(be_ref, bv_ref, first_ref, slot_ref, nxt_ref, x_ref, wug_hbm, bug_ref, wd_hbm, bd_ref, y_ref,
                    wug_f32, wd_f32, wug_bf, wd_bf, sems):
    i = pl.program_id(0)
    f = wd_bf.shape[0]

    def weight_copies(expert, slot):
        return (pltpu.make_async_copy(wug_hbm.at[expert], wug_f32.at[slot], sems.at[slot, 0]),
                pltpu.make_async_copy(wd_hbm.at[expert], wd_f32.at[slot], sems.at[slot, 1]))

    @pl.when(i == 0)
    def _():
        for cp in weight_copies(be_ref[0], 0):
            cp.start()

    @pl.when(first_ref[i] == 1)
    def _():
        slot = slot_ref[i]
        for cp in weight_copies(be_ref[i], slot):
            cp.wait()
        wug_bf[...] = wug_f32[slot].astype(BF16)
        wd_bf[...] = wd_f32[slot].astype(BF16)

        @pl.when(nxt_ref[i] >= 0)
        def _():
            for cp in weight_copies(nxt_ref[i], 1 - slot):
                cp.start()

    @pl.when(bv_ref[i] > 0)
    def _():
        xw = x_ref[...]
        row = lax.broadcasted_iota(jnp.int32, xw.shape, 0)
        first, second = _unpack_bf16_pairs(jnp.where(row < bv_ref[i], xw, jnp.uint32(0)))
        x = jnp.concatenate([first.astype(BF16), second.astype(BF16)], axis=1)
        gu = jnp.dot(x, wug_bf[...], preferred_element_type=F32) + bug_ref[...]
        gate = jnp.minimum(gu[:, :f], SWIGLU_LIMIT)
        up = jnp.clip(gu[:, f:], -SWIGLU_LIMIT, SWIGLU_LIMIT)
        act = (up + 1.0) * (gate * (1.0 / (1.0 + jnp.exp(-SWIGLU_ALPHA * gate))))
        y = jnp.dot(act.astype(BF16), wd_bf[...], preferred_element_type=F32) + bd_ref[...]
        y_ref[...] = _pack_bf16_pairs(y)

    @pl.when(bv_ref[i] <= 0)
    def _():
        y_ref[...] = jnp.zeros_like(y_ref)


def _experts(xs_sorted, blk_exp, blk_valid, w_up_gate, b_up_gate, w_down, b_down, tr):
    rows_total, w = xs_sorted.shape
    e, d, f2 = w_up_gate.shape
    f = f2 // 2
    n_tiles = rows_total // tr
    changed = jnp.concatenate([jnp.ones((1,), jnp.int32), (blk_exp[1:] != blk_exp[:-1]).astype(jnp.int32)])
    slot = (jnp.cumsum(changed) - 1) % 2
    tile_ids = jnp.arange(n_tiles, dtype=jnp.int32)
    later_first = (changed[None, :] == 1) & (tile_ids[None, :] > tile_ids[:, None])
    nxt_tile = jnp.min(jnp.where(later_first, tile_ids[None, :], n_tiles), axis=1)
    nxt_exp = jnp.where(nxt_tile < n_tiles, blk_exp[jnp.minimum(nxt_tile, n_tiles - 1)], -1).astype(jnp.int32)
    row_map = lambda i, *_: (i, 0)
    exp_map = lambda i, be, *_: (be[i], 0, 0)
    grid_spec = pltpu.PrefetchScalarGridSpec(
        num_scalar_prefetch=5,
        grid=(n_tiles,),
        in_specs=[
            pl.BlockSpec((tr, w), row_map),
            pl.BlockSpec(memory_space=pl.ANY),
            pl.BlockSpec((None, 1, f2), exp_map),
            pl.BlockSpec(memory_space=pl.ANY),
            pl.BlockSpec((None, 1, d), exp_map),
        ],
        out_specs=pl.BlockSpec((tr, d // 2), row_map),
        scratch_shapes=[
            pltpu.VMEM((2, d, f2), F32), pltpu.VMEM((2, f, d), F32),
            pltpu.VMEM((d, f2), BF16), pltpu.VMEM((f, d), BF16),
            pltpu.SemaphoreType.DMA((2, 2)),
        ],
    )
    return pl.pallas_call(
        _experts_kernel,
        grid_spec=grid_spec,
        out_shape=jax.ShapeDtypeStruct((rows_total, d // 2), jnp.uint32),
        compiler_params=pltpu.CompilerParams(dimension_semantics=("arbitrary",), vmem_limit_bytes=VMEM_LIMIT_BYTES),
        name="experts",
    )(blk_exp, blk_valid, changed, slot.astype(jnp.int32), nxt_exp,
      xs_sorted, w_up_gate, b_up_gate.reshape(e, 1, f2), w_down, b_down.reshape(e, 1, d))


def _combine_kernel(pos_ref, posn_ref, h_ref, gates_ref, ys_ref, o_ref, buf, sems):
    i = pl.program_id(0)
    n_steps = pl.num_programs(0)
    tm = h_ref.shape[0]
    slot = i % 2

    def row_copy(p_ref, t, k, s):
        p = p_ref[0, 0, t * TOP_K + k]
        return pltpu.make_async_copy(ys_ref.at[pl.ds(p, 1), :], buf.at[s, k, pl.ds(t, 1), :], sems.at[s, k])

    def issue(p_ref, s):
        def body(t, c):
            for k in range(TOP_K):
                row_copy(p_ref, t, k, s).start(priority=k % 2)
            return c
        lax.fori_loop(0, tm, body, 0, unroll=DMA_ISSUE_UNROLL)

    @pl.when(i == 0)
    def _():
        issue(pos_ref, 0)

    @pl.when(i + 1 < n_steps)
    def _():
        issue(posn_ref, 1 - slot)

    for k in range(TOP_K):
        pltpu.make_async_copy(ys_ref.at[pl.ds(0, tm), :], buf.at[slot, k], sems.at[slot, k]).wait()

    acc = h_ref[...]
    gates = gates_ref[...]
    for k in range(TOP_K):
        first, second = _unpack_bf16_pairs(buf[slot, k])
        acc = acc + gates[:, k:k + 1] * jnp.concatenate([first, second], axis=1)
    o_ref[...] = acc


def _combine(h, gates, pos, ys):
    n, d = h.shape
    tm = min(256, n)
    nt = n // tm
    pos3 = pos.reshape(nt, 1, tm * TOP_K)
    smem_blk = lambda imap: pl.BlockSpec((1, 1, tm * TOP_K), imap, memory_space=pltpu.SMEM)
    return pl.pallas_call(
        _combine_kernel,
        grid=(nt,),
        in_specs=[
            smem_blk(lambda i: (i, 0, 0)),
            smem_blk(lambda i: (jnp.minimum(i + 1, nt - 1), 0, 0)),
            pl.BlockSpec((tm, d), lambda i: (i, 0)),
            pl.BlockSpec((tm, LANES), lambda i: (i, 0)),
            pl.BlockSpec(memory_space=pl.ANY),
        ],
        out_specs=pl.BlockSpec((tm, d), lambda i: (i, 0)),
        out_shape=jax.ShapeDtypeStruct((n, d), F32),
        scratch_shapes=[pltpu.VMEM((2, TOP_K, tm, d // 2), jnp.uint32), pltpu.SemaphoreType.DMA((2, TOP_K))],
        compiler_params=pltpu.CompilerParams(dimension_semantics=("arbitrary",), vmem_limit_bytes=VMEM_LIMIT_BYTES),
        name="combine",
    )(pos3, pos3, h, gates, ys)


EXPERT_TILE = 512


def _layer(h2d, batch, seq, norm_mix, w_in, b_forget, q_norm_fox, k_norm_fox, q_norm_swa, k_norm_swa, sinks, rel_bias,
           w_out, norm_moe, w_router, b_router, w_up_gate, b_up_gate, w_down, b_down):
    n, d = h2d.shape
    n_experts = w_router.shape[1]
    qa, ka, fv, sq, sk, sv = _proj(h2d, norm_mix, w_in, b_forget, q_norm_fox, k_norm_fox, q_norm_swa, k_norm_swa, seq)
    fa = _fox(qa, ka, fv, batch, seq)
    sb = _swa(sq, sk, sv, sinks, rel_bias, batch, seq)
    h, xs_tok, meta, gates, counts = _outproj(h2d, fa, sb, w_out, norm_moe, w_router, b_router)

    tr = min(EXPERT_TILE, n)
    counts_i = counts[0, :n_experts].astype(jnp.int32)
    padded = (counts_i + tr - 1) // tr * tr
    pend = jnp.cumsum(padded)
    pstart = pend - padded
    idx = meta[:, :TOP_K]
    rank = meta[:, TOP_K:2 * TOP_K]
    onehot = idx[:, :, None] == jnp.arange(n_experts, dtype=jnp.int32)
    pos = rank + jnp.sum(jnp.where(onehot, pstart, 0), axis=-1)
    rows_total = n * TOP_K + n_experts * tr
    n_tiles = rows_total // tr
    blk_start = jnp.arange(n_tiles, dtype=jnp.int32) * tr
    blk_exp = jnp.minimum(jnp.sum(pend[None, :] <= blk_start[:, None], axis=1), n_experts - 1).astype(jnp.int32)
    blk_valid = jnp.clip(pstart[blk_exp] + counts_i[blk_exp] - blk_start, 0, tr).astype(jnp.int32)

    xs_sorted = _dispatch(xs_tok, pos.astype(jnp.int32), rows_total)
    ys = _experts(xs_sorted, blk_exp, blk_valid, w_up_gate, b_up_gate, w_down, b_down, tr)
    return _combine(h, gates, pos.astype(jnp.int32), ys)


def kernel(x, norm_mix, w_in, b_forget, q_norm_fox, k_norm_fox, q_norm_swa, k_norm_swa, sinks, rel_bias, w_out, norm_moe,
           w_router, b_router, w_up_gate, b_up_gate, w_down, b_down):
    batch, seq, d = x.shape
    h = x.reshape(batch * seq, d)
    for l in range(norm_mix.shape[0]):
        h = _layer(h, batch, seq, norm_mix[l], w_in[l], b_forget[l], q_norm_fox[l], k_norm_fox[l], q_norm_swa[l],
                   k_norm_swa[l], sinks[l], rel_bias, w_out[l], norm_moe[l], w_router[l], b_router[l], w_up_gate[l],
                   b_up_gate[l], w_down[l], b_down[l])
    return h.reshape(batch, seq, d)
```

```python
import functools
import math

import numpy as np
import jax
import jax.numpy as jnp
from jax import lax
from jax.experimental import pallas as pl
from jax.experimental.pallas import tpu as pltpu

HEAD_DIM = 64
FOX_HEADS = 8
SWA_HEADS = 8
SWA_KV_HEADS = 2
SWA_GROUP = SWA_HEADS // SWA_KV_HEADS
WINDOW = 128
N_BUCKETS = 32
MAX_DISTANCE = 128
TOP_K = 4
SWIGLU_LIMIT = 7.0
SWIGLU_ALPHA = 1.702
EPS = 1e-6

LANES = 128
VMEM_LIMIT_BYTES = 56 * 1024 * 1024
NEG_INF = float("-inf")
LOG2E = math.log2(math.e)

F32 = jnp.float32
BF16 = jnp.bfloat16


def _lane_iota(shape):
    return lax.broadcasted_iota(jnp.int32, shape, len(shape) - 1)


def _split3_bf16(v):
    hi = v.astype(BF16)
    r1 = v - hi.astype(F32)
    mid = r1.astype(BF16)
    lo = (r1 - mid.astype(F32)).astype(BF16)
    return hi, mid, lo


def _pack_bf16_pairs(v):
    w = v.shape[1] // 2
    u = pltpu.bitcast(v.astype(BF16).astype(F32), jnp.uint32)
    return u[:, :w] | (u[:, w:] >> 16)


def _unpack_bf16_pairs(u):
    first = pltpu.bitcast(u & jnp.uint32(0xFFFF0000), F32)
    second = pltpu.bitcast(u << 16, F32)
    return first, second


CUM_CHUNK = 128


def _proj_kernel(x_ref, g_ref, wqa_ref, wka_ref, wrest_ref, bf_ref, gqa_ref, gka_ref, gsq_ref, gsk_ref, tri_ref,
                 qa_ref, ka_ref, fv_ref, sq_ref, sk_ref, sv_ref, carry_ref, *, tiles_per_seq):
    i = pl.program_id(0)
    tm = x_ref.shape[0]
    nh = FOX_HEADS
    scale = HEAD_DIM ** -0.5

    @pl.when(i % tiles_per_seq == 0)
    def _():
        carry_ref[...] = jnp.zeros_like(carry_ref)

    x = x_ref[...]
    ms = jnp.mean(x * x, axis=-1, keepdims=True)
    hn = (x * lax.rsqrt(ms + EPS) * g_ref[...]).astype(BF16)

    qa = jnp.dot(hn, wqa_ref[...], preferred_element_type=F32)
    ka = jnp.dot(hn, wka_ref[...], preferred_element_type=F32)
    rest = jnp.dot(hn, wrest_ref[...], preferred_element_type=F32)

    lane_w = _lane_iota((tm, nh * LANES)) % LANES
    upper = lane_w >= HEAD_DIM

    fl = qa + bf_ref[...]
    ls = jnp.minimum(fl, 0.0) - jnp.log(1.0 + jnp.exp(-jnp.abs(fl)))
    ls = jnp.where(upper, ls, 0.0)
    tri = tri_ref[...]
    carry = carry_ref[0:1, :]
    c_chunks = []
    for r in range(tm // CUM_CHUNK):
        hi, mid, lo = _split3_bf16(ls[r * CUM_CHUNK:(r + 1) * CUM_CHUNK, :])
        cs = jnp.dot(tri, jnp.concatenate([hi, mid, lo], axis=0), preferred_element_type=F32) + carry
        carry = cs[CUM_CHUNK - 1:CUM_CHUNK, :]
        c_chunks.append(cs)
    carry_ref[0:1, :] = carry
    c = jnp.concatenate(c_chunks, axis=0) * LOG2E

    c1 = c.astype(BF16).astype(F32)
    c2 = (c - c1).astype(BF16).astype(F32)
    c3 = c - c1 - c2

    def head_norm(t, g_row):
        outs = []
        for h in range(t.shape[1] // LANES):
            th = t[:, h * LANES:(h + 1) * LANES]
            lo_mask = _lane_iota(th.shape) < HEAD_DIM
            ss = jnp.sum(jnp.where(lo_mask, th * th, 0.0), axis=-1, keepdims=True)
            outs.append(th * lax.rsqrt(ss * (1.0 / HEAD_DIM) + EPS))
        return jnp.concatenate(outs, axis=1) * g_row

    qn = head_norm(qa, gqa_ref[...]) * (scale * LOG2E)
    kn = head_norm(ka, gka_ref[...])

    one = jnp.float32(1.0)
    zero = jnp.float32(0.0)
    q_aug = jnp.where(lane_w == 64, c1, jnp.where(lane_w == 65, c2, jnp.where(lane_w == 66, c3,
                      jnp.where(lane_w < 70, one, zero))))
    k_aug = jnp.where(lane_w < 67, one, jnp.where(lane_w == 67, -c1, jnp.where(lane_w == 68, -c2,
                      jnp.where(lane_w == 69, -c3, zero))))
    qa_ref[...] = jnp.where(upper, q_aug, qn).astype(BF16)
    ka_ref[...] = jnp.where(upper, k_aug, kn).astype(BF16)

    nfv = FOX_HEADS * HEAD_DIM
    nsq = SWA_HEADS * HEAD_DIM
    nsk = SWA_KV_HEADS * HEAD_DIM
    fv_ref[...] = rest[:, :nfv].astype(BF16)

    def half_norm(t, g_row):
        outs = []
        for j in range(t.shape[1] // LANES):
            tj = t[:, j * LANES:(j + 1) * LANES]
            lo_mask = _lane_iota(tj.shape) < HEAD_DIM
            sq = tj * tj
            s_lo = jnp.sum(jnp.where(lo_mask, sq, 0.0), axis=-1, keepdims=True)
            s_hi = jnp.sum(jnp.where(lo_mask, 0.0, sq), axis=-1, keepdims=True)
            r = jnp.where(lo_mask, lax.rsqrt(s_lo * (1.0 / HEAD_DIM) + EPS), lax.rsqrt(s_hi * (1.0 / HEAD_DIM) + EPS))
            outs.append(tj * r)
        return jnp.concatenate(outs, axis=1) * g_row

    sq_ref[...] = (half_norm(rest[:, nfv:nfv + nsq], gsq_ref[...]) * scale).astype(BF16)
    sk_ref[...] = half_norm(rest[:, nfv + nsq:nfv + nsq + nsk], gsk_ref[...]).astype(BF16)
    sv_ref[...] = rest[:, nfv + nsq + nsk:].astype(BF16)


def _proj(x2d, norm_mix, w_in, b_forget, q_norm_fox, k_norm_fox, q_norm_swa, k_norm_swa, seq):
    n, d = x2d.shape
    nh, dh = FOX_HEADS, HEAD_DIM
    tm = min(512, seq)
    assert seq % tm == 0 and tm % CUM_CHUNK == 0
    points = np.cumsum([nh * dh, nh * dh, nh * dh, nh, SWA_HEADS * dh, SWA_KV_HEADS * dh, SWA_KV_HEADS * dh])
    wfq, wfk, wfv, wff, wsq, wsk, wsv = jnp.split(w_in, [int(p) for p in points[:-1]], axis=1)

    wfq = wfq.reshape(d, nh, dh)
    wff_rep = jnp.broadcast_to(wff[:, :, None], (d, nh, dh))
    wqa = jnp.concatenate([wfq, wff_rep], axis=2).reshape(d, nh * LANES).astype(BF16)
    wka = jnp.concatenate([wfk.reshape(d, nh, dh), jnp.zeros((d, nh, dh), w_in.dtype)], axis=2)
    wka = wka.reshape(d, nh * LANES).astype(BF16)
    wsq = wsq.reshape(d, SWA_KV_HEADS, SWA_GROUP, dh).transpose(0, 2, 1, 3).reshape(d, SWA_HEADS * dh)
    wrest = jnp.concatenate([wfv, wsq, wsk, wsv], axis=1).astype(BF16)

    zeros_h = jnp.zeros((nh, dh), F32)
    bf_row = jnp.concatenate([zeros_h, jnp.broadcast_to(b_forget[:, None], (nh, dh))], axis=1).reshape(1, nh * LANES)
    gqa = jnp.concatenate([jnp.broadcast_to(q_norm_fox[None], (nh, dh)), zeros_h], axis=1).reshape(1, nh * LANES)
    gka = jnp.concatenate([jnp.broadcast_to(k_norm_fox[None], (nh, dh)), zeros_h], axis=1).reshape(1, nh * LANES)
    gsq = jnp.tile(q_norm_swa, SWA_HEADS).reshape(1, SWA_HEADS * dh)
    gsk = jnp.tile(k_norm_swa, SWA_KV_HEADS).reshape(1, SWA_KV_HEADS * dh)
    tri = np.tril(np.ones((CUM_CHUNK, CUM_CHUNK), np.float32))
    tri3 = jnp.asarray(np.concatenate([tri, tri, tri], axis=1), BF16)

    nrest = wrest.shape[1]
    const = lambda i: (0, 0)
    row = lambda i: (i, 0)
    out_shapes = (
        jax.ShapeDtypeStruct((n, nh * LANES), BF16),
        jax.ShapeDtypeStruct((n, nh * LANES), BF16),
        jax.ShapeDtypeStruct((n, nh * dh), BF16),
        jax.ShapeDtypeStruct((n, SWA_HEADS * dh), BF16),
        jax.ShapeDtypeStruct((n, SWA_KV_HEADS * dh), BF16),
        jax.ShapeDtypeStruct((n, SWA_KV_HEADS * dh), BF16),
    )
    return pl.pallas_call(
        functools.partial(_proj_kernel, tiles_per_seq=seq // tm),
        grid=(n // tm,),
        in_specs=[
            pl.BlockSpec((tm, d), row),
            pl.BlockSpec((1, d), const),
            pl.BlockSpec((d, nh * LANES), const),
            pl.BlockSpec((d, nh * LANES), const),
            pl.BlockSpec((d, nrest), const),
            pl.BlockSpec((1, nh * LANES), const),
            pl.BlockSpec((1, nh * LANES), const),
            pl.BlockSpec((1, nh * LANES), const),
            pl.BlockSpec((1, SWA_HEADS * dh), const),
            pl.BlockSpec((1, SWA_KV_HEADS * dh), const),
            pl.BlockSpec((CUM_CHUNK, 3 * CUM_CHUNK), const),
        ],
        out_specs=[
            pl.BlockSpec((tm, nh * LANES), row),
            pl.BlockSpec((tm, nh * LANES), row),
            pl.BlockSpec((tm, nh * dh), row),
            pl.BlockSpec((tm, SWA_HEADS * dh), row),
            pl.BlockSpec((tm, SWA_KV_HEADS * dh), row),
            pl.BlockSpec((tm, SWA_KV_HEADS * dh), row),
        ],
        out_shape=out_shapes,
        scratch_shapes=[pltpu.VMEM((8, nh * LANES), F32)],
        compiler_params=pltpu.CompilerParams(dimension_semantics=("arbitrary",), vmem_limit_bytes=VMEM_LIMIT_BYTES),
        name="proj",
    )(x2d, norm_mix.reshape(1, d), wqa, wka, wrest, bf_row, gqa, gka, gsq, gsk, tri3)


FOX_BLOCK = 512


def _fox_kernel(qa_ref, ka_ref, v_ref, o_ref, s0_sc, s1_sc, m_sc, l_sc, acc_sc, *, blk):
    i = pl.program_id(2)
    nt = (((1,), (1,)), ((), ()))
    row = lax.broadcasted_iota(jnp.int32, (blk, blk), 0)
    col = lax.broadcasted_iota(jnp.int32, (blk, blk), 1)
    causal = col <= row
    s_bufs = (s0_sc, s1_sc)

    def lane_tiles(s):
        return [s[:, c * LANES:(c + 1) * LANES] for c in range(s.shape[1] // LANES)]

    def row_max(tiles):
        m = jnp.max(functools.reduce(jnp.maximum, tiles), axis=-1, keepdims=True)
        return jnp.broadcast_to(m, (blk, LANES))

    def row_sum(tiles):
        t = jnp.sum(functools.reduce(jnp.add, tiles), axis=-1, keepdims=True)
        return jnp.broadcast_to(t, (blk, LANES))

    def produce_scores(j, slot):
        start = pl.multiple_of(j * blk, blk)
        for hh in range(2):
            q = qa_ref[:, hh * LANES:(hh + 1) * LANES]
            k = ka_ref[pl.ds(start, blk), hh * LANES:(hh + 1) * LANES]
            s_bufs[slot][hh] = lax.dot_general(q, k, nt, preferred_element_type=F32)

    def consume_scores(j, slot, masked):
        v = v_ref[pl.ds(pl.multiple_of(j * blk, blk), blk), :]
        for hh in range(2):
            s = s_bufs[slot][hh]
            if masked:
                s = jnp.where(causal, s, NEG_INF)
            tiles = lane_tiles(s)
            m = m_sc[hh]
            m_new = jnp.maximum(m, row_max(tiles))
            alpha = jnp.exp2(m - m_new)
            p_tiles = [jnp.exp2(t - m_new) for t in tiles]
            p = jnp.concatenate(p_tiles, axis=1).astype(BF16)
            m_sc[hh] = m_new
            l_sc[hh] = alpha * l_sc[hh] + row_sum(p_tiles)
            acc_sc[hh] = alpha * acc_sc[hh] + jnp.dot(p, v, preferred_element_type=F32)

    m_sc[...] = jnp.full(m_sc.shape, NEG_INF, F32)
    l_sc[...] = jnp.zeros(l_sc.shape, F32)
    acc_sc[...] = jnp.zeros(acc_sc.shape, F32)
    produce_scores(0, 0)

    def pair(jj, c):
        j = 2 * jj
        produce_scores(j + 1, 1)
        consume_scores(j, 0, masked=False)
        produce_scores(j + 2, 0)
        consume_scores(j + 1, 1, masked=False)
        return c

    lax.fori_loop(0, i // 2, pair, 0)

    @pl.when(i % 2 == 0)
    def _():
        consume_scores(i, 0, masked=True)

    @pl.when(i % 2 == 1)
    def _():
        produce_scores(i, 1)
        consume_scores(i - 1, 0, masked=False)
        consume_scores(i, 1, masked=True)

    out0 = acc_sc[0] / l_sc[0]
    out1 = acc_sc[1] / l_sc[1]
    lane = _lane_iota(out0.shape)
    o_ref[...] = jnp.where(lane < HEAD_DIM, out0, out1).astype(o_ref.dtype)


def _fox(qa, ka, fv, batch, seq):
    nh, dh = FOX_HEADS, HEAD_DIM
    blk = min(FOX_BLOCK, seq)
    assert seq % blk == 0
    qa3 = qa.reshape(batch, seq, nh * LANES)
    ka3 = ka.reshape(batch, seq, nh * LANES)
    fv3 = fv.reshape(batch, seq, nh * dh)
    out = pl.pallas_call(
        functools.partial(_fox_kernel, blk=blk),
        grid=(batch, nh // 2, seq // blk),
        in_specs=[
            pl.BlockSpec((None, blk, 2 * LANES), lambda b, j, i: (b, i, j)),
            pl.BlockSpec((None, seq, 2 * LANES), lambda b, j, i: (b, 0, j)),
            pl.BlockSpec((None, seq, LANES), lambda b, j, i: (b, 0, j)),
        ],
        out_specs=pl.BlockSpec((None, blk, LANES), lambda b, j, i: (b, i, j)),
        out_shape=jax.ShapeDtypeStruct((batch, seq, nh * dh), BF16),
        scratch_shapes=[
            pltpu.VMEM((2, blk, blk), F32),
            pltpu.VMEM((2, blk, blk), F32),
            pltpu.VMEM((2, blk, LANES), F32),
            pltpu.VMEM((2, blk, LANES), F32),
            pltpu.VMEM((2, blk, LANES), F32),
        ],
        compiler_params=pltpu.CompilerParams(dimension_semantics=("arbitrary", "arbitrary", "arbitrary"),
                                             vmem_limit_bytes=VMEM_LIMIT_BYTES),
        name="fox",
    )(qa3, ka3, fv3)
    return out.reshape(batch * seq, nh * dh)


def _t5_bucket_table():
    qi = np.arange(WINDOW)[:, None]
    kj = np.arange(2 * WINDOW)[None, :]
    dist = qi + WINDOW - kj
    n = np.maximum(dist, 0)
    max_exact = N_BUCKETS // 2
    nf = np.maximum(n, 1).astype(np.float64)
    large = max_exact + (np.log(nf / max_exact) / math.log(MAX_DISTANCE / max_exact) * (N_BUCKETS - max_exact)).astype(np.int64)
    large = np.minimum(large, N_BUCKETS - 1)
    bucket = np.where(n < max_exact, n, large)
    valid = (dist >= 0) & (dist < WINDOW)
    return np.where(valid, bucket, -1).astype(np.int32)


def _swa_kernel(rb_ref, sinks_ref, bucket_ref, sq_ref, sk_ref, sv_ref, skp_ref, svp_ref, o_ref, bias_ref, *, blocks_per_tile):
    b = pl.program_id(0)
    i = pl.program_id(1)
    w = WINDOW
    g = SWA_GROUP

    @pl.when((b == 0) & (i == 0))
    def _():
        bucket = bucket_ref[...]
        for h in range(SWA_HEADS):
            acc = jnp.full((w, 2 * w), NEG_INF, F32)
            for bk in range(N_BUCKETS):
                acc = jnp.where(bucket == bk, rb_ref[bk, h], acc)
            kv, gi = divmod(h, g)
            bias_ref[kv, gi * w:(gi + 1) * w, :] = acc

    nt = (((1,), (1,)), ((), ()))
    lane = _lane_iota((w, LANES))
    col = _lane_iota((g * w, 2 * w))
    for cb in range(blocks_per_tile):
        rows = slice(cb * w, (cb + 1) * w)
        if cb == 0:
            kprev, vprev = skp_ref[...], svp_ref[...]
        else:
            kprev, vprev = sk_ref[(cb - 1) * w:cb * w, :], sv_ref[(cb - 1) * w:cb * w, :]
        kcat = jnp.concatenate([kprev, sk_ref[rows, :]], axis=0)
        vcat = jnp.concatenate([vprev, sv_ref[rows, :]], axis=0)
        outs = []
        for kv in range(SWA_KV_HEADS):
            keep = (lane < HEAD_DIM) if kv == 0 else (lane >= HEAD_DIM)
            qs = [jnp.where(keep, sq_ref[rows, gi * LANES:(gi + 1) * LANES], jnp.zeros((), BF16)) for gi in range(g)]
            qstack = jnp.concatenate(qs, axis=0)
            s = lax.dot_general(qstack, kcat, nt, preferred_element_type=F32) + bias_ref[kv]
            if cb == 0:
                s = s + jnp.where(col < w, jnp.where(i == 0, NEG_INF, 0.0), 0.0)
            sink = jnp.concatenate([jnp.full((w, 1), sinks_ref[kv * g + gi], F32) for gi in range(g)], axis=0)
            m = jnp.maximum(jnp.max(s, axis=-1, keepdims=True), sink)
            p = jnp.exp(s - m)
            denom = jnp.sum(p, axis=-1, keepdims=True) + jnp.exp(sink - m)
            outs.append(jnp.dot(p.astype(BF16), vcat, preferred_element_type=F32) / denom)
        for gi in range(g):
            o_ref[rows, gi * LANES:(gi + 1) * LANES] = jnp.where(
                lane < HEAD_DIM, outs[0][gi * w:(gi + 1) * w, :], outs[1][gi * w:(gi + 1) * w, :]).astype(o_ref.dtype)


def _swa(sq, sk, sv, sinks, rel_bias, batch, seq):
    w = WINDOW
    dh = HEAD_DIM
    blocks_per_tile = min(4, seq // w)
    t = blocks_per_tile * w
    assert seq % t == 0
    sq3 = sq.reshape(batch, seq, SWA_HEADS * dh)
    sk3 = sk.reshape(batch, seq, SWA_KV_HEADS * dh)
    sv3 = sv.reshape(batch, seq, SWA_KV_HEADS * dh)
    bucket = jnp.asarray(_t5_bucket_table())
    cur = lambda b, i: (b, i, 0)
    prev = lambda b, i: (b, jnp.maximum(i * blocks_per_tile - 1, 0), 0)
    smem = pl.BlockSpec(memory_space=pltpu.SMEM)
    out = pl.pallas_call(
        functools.partial(_swa_kernel, blocks_per_tile=blocks_per_tile),
        grid=(batch, seq // t),
        in_specs=[
            smem, smem,
            pl.BlockSpec((w, 2 * w), lambda b, i: (0, 0)),
            pl.BlockSpec((None, t, SWA_HEADS * dh), cur),
            pl.BlockSpec((None, t, SWA_KV_HEADS * dh), cur),
            pl.BlockSpec((None, t, SWA_KV_HEADS * dh), cur),
            pl.BlockSpec((None, w, SWA_KV_HEADS * dh), prev),
            pl.BlockSpec((None, w, SWA_KV_HEADS * dh), prev),
        ],
        out_specs=pl.BlockSpec((None, t, SWA_HEADS * dh), cur),
        out_shape=jax.ShapeDtypeStruct((batch, seq, SWA_HEADS * dh), BF16),
        scratch_shapes=[pltpu.VMEM((SWA_KV_HEADS, SWA_GROUP * w, 2 * w), F32)],
        compiler_params=pltpu.CompilerParams(dimension_semantics=("arbitrary", "arbitrary"),
                                             vmem_limit_bytes=VMEM_LIMIT_BYTES),
        name="swa",
    )(rel_bias, sinks, bucket, sq3, sk3, sv3, sk3, sv3)
    return out.reshape(batch * seq, SWA_HEADS * dh)


SEG_ALIGN = 8
SEL_ROWS = 32
PERM_CHUNK = 256


def _local_rows(tm, n_experts):
    return -(-(tm * TOP_K + n_experts * SEG_ALIGN) // PERM_CHUNK) * PERM_CHUNK


def _outproj_kernel(x_ref, fa_ref, sb_ref, wa_ref, wb_ref, g_ref, wrh_ref, wrl_ref, br_ref, lstrict_ref, ustrict_ref,
                    sel_ref, h_ref, hn_ref, apos_ref, rowst_ref, tab_ref, carry_ref, *, n_experts):
    i = pl.program_id(0)
    tm = x_ref.shape[0]

    @pl.when(i == 0)
    def _():
        carry_ref[...] = jnp.zeros_like(carry_ref)

    h = (x_ref[...] + jnp.dot(fa_ref[...], wa_ref[...], preferred_element_type=F32)
         + jnp.dot(sb_ref[...], wb_ref[...], preferred_element_type=F32))
    h_ref[...] = h
    ms = jnp.mean(h * h, axis=-1, keepdims=True)
    hn = h * lax.rsqrt(ms + EPS) * g_ref[...]
    hn_ref[...] = hn.astype(BF16)

    hn_hi = hn.astype(BF16)
    hn_lo = (hn - hn_hi.astype(F32)).astype(BF16)
    wrh = wrh_ref[...]
    logits = (jnp.dot(hn_hi, wrh, preferred_element_type=F32) + jnp.dot(hn_lo, wrh, preferred_element_type=F32)
              + jnp.dot(hn_hi, wrl_ref[...], preferred_element_type=F32) + br_ref[...])
    lane = _lane_iota((tm, LANES))
    lane_f = lane.astype(F32)
    l = jnp.where(lane < n_experts, logits, NEG_INF)
    vals, idxs = [], []
    member = jnp.zeros((tm, LANES), F32)
    for _ in range(TOP_K):
        mk = jnp.max(l, axis=-1, keepdims=True)
        ik = jnp.min(jnp.where(l == mk, lane_f, float(LANES)), axis=-1, keepdims=True)
        sel = lane_f == ik
        l = jnp.where(sel, NEG_INF, l)
        member = jnp.where(sel, 1.0, member)
        vals.append(mk)
        idxs.append(ik)
    es = [jnp.exp(v - vals[0]) for v in vals]
    tot = es[0] + es[1] + es[2] + es[3]
    gates = [e / tot for e in es]

    lrank = jnp.dot(lstrict_ref[...], member.astype(BF16), preferred_element_type=F32)
    cnt = lrank[tm - 1:tm, :] + member[tm - 1:tm, :]
    cnt_al = jnp.floor((cnt + (SEG_ALIGN - 1)) * (1.0 / SEG_ALIGN)) * SEG_ALIGN
    base = carry_ref[0:1, :]
    carry_ref[0:1, :] = base + cnt_al
    astart = jnp.dot(jnp.broadcast_to(cnt_al, (8, LANES)).astype(BF16), ustrict_ref[...],
                     preferred_element_type=F32)[0:1, :]
    local = astart + lrank
    apos = [jnp.sum(jnp.where(lane_f == ik, local, 0.0), axis=-1, keepdims=True) for ik in idxs]

    tab_ref[...] = jnp.concatenate([base, cnt, astart, jnp.zeros((5, LANES), F32)], axis=0)
    ap = jnp.zeros((tm, LANES), F32)
    for k in range(TOP_K):
        ap = jnp.where(lane == k, apos[k], ap)
    apos_ref[...] = ap

    cols = jnp.zeros((tm, LANES), F32)
    for k in range(TOP_K):
        hi = jnp.floor(apos[k] * (1.0 / 64.0))
        cols = jnp.where(lane == 2 * k, hi, cols)
        cols = jnp.where(lane == 2 * k + 1, apos[k] - 64.0 * hi, cols)
        g1 = gates[k].astype(BF16).astype(F32)
        g2 = (gates[k] - g1).astype(BF16).astype(F32)
        cols = jnp.where(lane == 2 * TOP_K + 3 * k, g1, cols)
        cols = jnp.where(lane == 2 * TOP_K + 3 * k + 1, g2, cols)
        cols = jnp.where(lane == 2 * TOP_K + 3 * k + 2, gates[k] - g1 - g2, cols)
    rowst_ref[...] = lax.dot_general(sel_ref[...], cols.astype(BF16), (((1,), (1,)), ((), ())),
                                     preferred_element_type=F32)


def _outproj(x2d, fa, sb, w_out, norm_moe, w_router, b_router):
    n, d = x2d.shape
    n_experts = w_router.shape[1]
    assert n_experts <= LANES
    tm = min(512, n)
    assert n % tm == 0
    nfa = FOX_HEADS * HEAD_DIM
    wa = w_out[:nfa].astype(BF16)
    wb = w_out[nfa:].reshape(SWA_KV_HEADS, SWA_GROUP, HEAD_DIM, d).transpose(1, 0, 2, 3).reshape(SWA_HEADS * HEAD_DIM, d)
    wb = wb.astype(BF16)
    wr = jnp.zeros((d, LANES), F32).at[:, :n_experts].set(w_router)
    wr_hi = wr.astype(BF16)
    wr_lo = (wr - wr_hi.astype(F32)).astype(BF16)
    br = jnp.zeros((1, LANES), F32).at[0, :n_experts].set(b_router)
    lstrict = jnp.asarray(np.tril(np.ones((tm, tm), np.float32), -1), BF16)
    ustrict = jnp.asarray(np.triu(np.ones((LANES, LANES), np.float32), 1), BF16)
    sel = jnp.asarray(np.eye(SEL_ROWS, LANES, dtype=np.float32), BF16)
    assert 2 * TOP_K + 3 * TOP_K <= SEL_ROWS and _local_rows(tm, n_experts) <= 64 * 256
    const = lambda i: (0, 0)
    row = lambda i: (i, 0)
    return pl.pallas_call(
        functools.partial(_outproj_kernel, n_experts=n_experts),
        grid=(n // tm,),
        in_specs=[
            pl.BlockSpec((tm, d), row),
            pl.BlockSpec((tm, nfa), row),
            pl.BlockSpec((tm, SWA_HEADS * HEAD_DIM), row),
            pl.BlockSpec((nfa, d), const),
            pl.BlockSpec((SWA_HEADS * HEAD_DIM, d), const),
            pl.BlockSpec((1, d), const),
            pl.BlockSpec((d, LANES), const),
            pl.BlockSpec((d, LANES), const),
            pl.BlockSpec((1, LANES), const),
            pl.BlockSpec((tm, tm), const),
            pl.BlockSpec((LANES, LANES), const),
            pl.BlockSpec((SEL_ROWS, LANES), const),
        ],
        out_specs=[
            pl.BlockSpec((tm, d), row),
            pl.BlockSpec((tm, d), row),
            pl.BlockSpec((tm, LANES), row),
            pl.BlockSpec((SEL_ROWS, tm), lambda i: (0, i)),
            pl.BlockSpec((8, LANES), row),
        ],
        out_shape=(
            jax.ShapeDtypeStruct((n, d), F32),
            jax.ShapeDtypeStruct((n, d), BF16),
            jax.ShapeDtypeStruct((n, LANES), F32),
            jax.ShapeDtypeStruct((SEL_ROWS, n), F32),
            jax.ShapeDtypeStruct((n // tm * 8, LANES), F32),
        ),
        scratch_shapes=[pltpu.VMEM((8, LANES), F32)],
        compiler_params=pltpu.CompilerParams(dimension_semantics=("arbitrary",), vmem_limit_bytes=VMEM_LIMIT_BYTES),
        name="outproj",
    )(x2d, fa, sb, wa, wb, norm_moe.reshape(1, d), wr_hi, wr_lo, br, lstrict, ustrict, sel)


def _dispatch_kernel(ast_ref, gdst_ref, nch_ref, hn_ref, rowst_ref, xs_out, gs_out, perm, loc_x, loc_g, sems, *,
                     n_experts):
    t_idx = pl.program_id(0)
    tm = hn_ref.shape[0]
    kloc = loc_x.shape[0]

    rows = rowst_ref[...]
    a_rows = [64.0 * rows[2 * k:2 * k + 1, :] + rows[2 * k + 1:2 * k + 2, :] for k in range(TOP_K)]
    g_rows = [rows[2 * TOP_K + 3 * k:2 * TOP_K + 3 * k + 1, :] + rows[2 * TOP_K + 3 * k + 1:2 * TOP_K + 3 * k + 2, :]
              + rows[2 * TOP_K + 3 * k + 2:2 * TOP_K + 3 * k + 3, :] for k in range(TOP_K)]

    for c in range(kloc // PERM_CHUNK):
        r_iota = (lax.broadcasted_iota(jnp.int32, (PERM_CHUNK, tm), 0) + c * PERM_CHUNK).astype(F32)
        hit = jnp.zeros((PERM_CHUNK, tm), F32)
        gw = jnp.zeros((PERM_CHUNK, tm), F32)
        for k in range(TOP_K):
            eq = r_iota == a_rows[k]
            hit = jnp.where(eq, 1.0, hit)
            gw = jnp.where(eq, g_rows[k], gw)
        perm[c * PERM_CHUNK:(c + 1) * PERM_CHUNK, :] = hit.astype(BF16)
        gsum = functools.reduce(jnp.add, [gw[:, j * LANES:(j + 1) * LANES] for j in range(tm // LANES)])
        loc_g[c * PERM_CHUNK:(c + 1) * PERM_CHUNK, :] = jnp.broadcast_to(
            jnp.sum(gsum, axis=-1, keepdims=True), (PERM_CHUNK, LANES))

    loc_x[...] = _pack_bf16_pairs(jnp.dot(perm[...], hn_ref[...], preferred_element_type=F32))

    def seg_copies(e, c):
        s0 = pl.multiple_of(ast_ref[t_idx * n_experts + e] + c * SEG_ALIGN, SEG_ALIGN)
        d0 = pl.multiple_of(gdst_ref[t_idx * n_experts + e] + c * SEG_ALIGN, SEG_ALIGN)
        return (pltpu.make_async_copy(loc_x.at[pl.ds(s0, SEG_ALIGN), :], xs_out.at[pl.ds(d0, SEG_ALIGN), :], sems.at[0]),
                pltpu.make_async_copy(loc_g.at[pl.ds(s0, SEG_ALIGN), :], gs_out.at[pl.ds(d0, SEG_ALIGN), :], sems.at[1]))

    def for_each_piece(fn):
        def per_expert(e, carry):
            def per_piece(c, carry):
                for q, cp in enumerate(seg_copies(e, c)):
                    fn(cp, q)
                return carry
            return lax.fori_loop(0, nch_ref[t_idx * n_experts + e], per_piece, carry)
        lax.fori_loop(0, n_experts, per_expert, 0)

    for_each_piece(lambda cp, q: cp.start(priority=q))
    for_each_piece(lambda cp, q: cp.wait())


def _dispatch(hn, rowst, ast, gdst, nch, rows_total, n_experts):
    n, d = hn.shape
    tm = min(512, n)
    kloc = _local_rows(tm, n_experts)
    assert kloc % PERM_CHUNK == 0
    grid_spec = pltpu.PrefetchScalarGridSpec(
        num_scalar_prefetch=3,
        grid=(n // tm,),
        in_specs=[
            pl.BlockSpec((tm, d), lambda i, *_: (i, 0)),
            pl.BlockSpec((SEL_ROWS, tm), lambda i, *_: (0, i)),
        ],
        out_specs=[pl.BlockSpec(memory_space=pl.ANY), pl.BlockSpec(memory_space=pl.ANY)],
        scratch_shapes=[
            pltpu.VMEM((kloc, tm), BF16),
            pltpu.VMEM((kloc, d // 2), jnp.uint32),
            pltpu.VMEM((kloc, LANES), F32),
            pltpu.SemaphoreType.DMA((2,)),
        ],
    )
    return pl.pallas_call(
        functools.partial(_dispatch_kernel, n_experts=n_experts),
        grid_spec=grid_spec,
        out_shape=(jax.ShapeDtypeStruct((rows_total, d // 2), jnp.uint32),
                   jax.ShapeDtypeStruct((rows_total, LANES), F32)),
        compiler_params=pltpu.CompilerParams(dimension_semantics=("arbitrary",), vmem_limit_bytes=VMEM_LIMIT_BYTES,
                                             has_side_effects=True),
        name="dispatch",
    )(ast, gdst, nch, hn, rowst)


def _experts_kernel(be_ref, bv_ref, first_ref, slot_ref, nxt_ref, x_ref, gs_ref, wug_hbm, bug_ref, wd_hbm, bd_ref, y_ref,
                    wug_f32, wd_f32, wug_bf, wd_bf, sems):
    i = pl.program_id(0)
    f = wd_bf.shape[0]

    def weight_copies(expert, slot):
        return (pltpu.make_async_copy(wug_hbm.at[expert], wug_f32.at[slot], sems.at[slot, 0]),
                pltpu.make_async_copy(wd_hbm.at[expert], wd_f32.at[slot], sems.at[slot, 1]))

    @pl.when(i == 0)
    def _():
        for cp in weight_copies(be_ref[0], 0):
            cp.start()

    @pl.when(first_ref[i] == 1)
    def _():
        slot = slot_ref[i]
        for cp in weight_copies(be_ref[i], slot):
            cp.wait()
        wug_bf[...] = wug_f32[slot].astype(BF16)
        wd_bf[...] = wd_f32[slot].astype(BF16)

        @pl.when(nxt_ref[i] >= 0)
        def _():
            for cp in weight_copies(nxt_ref[i], 1 - slot):
                cp.start()

    @pl.when(bv_ref[i] > 0)
    def _():
        xw = x_ref[...]
        row = lax.broadcasted_iota(jnp.int32, xw.shape, 0)
        first, second = _unpack_bf16_pairs(jnp.where(row < bv_ref[i], xw, jnp.uint32(0)))
        x = jnp.concatenate([first.astype(BF16), second.astype(BF16)], axis=1)
        gu = jnp.dot(x, wug_bf[...], preferred_element_type=F32) + bug_ref[...]
        gate = jnp.minimum(gu[:, :f], SWIGLU_LIMIT)
        up = jnp.clip(gu[:, f:], -SWIGLU_LIMIT, SWIGLU_LIMIT)
        act = (up + 1.0) * (gate * (1.0 / (1.0 + jnp.exp(-SWIGLU_ALPHA * gate))))
        y = jnp.dot(act.astype(BF16), wd_bf[...], preferred_element_type=F32) + bd_ref[...]
        gate_row = jnp.where(row[:, :LANES] < bv_ref[i], gs_ref[...], 0.0)
        y = y * jnp.concatenate([gate_row] * (y.shape[1] // LANES), axis=1)
        y_ref[...] = _pack_bf16_pairs(y)

    @pl.when(bv_ref[i] <= 0)
    def _():
        y_ref[...] = jnp.zeros_like(y_ref)


def _experts(xs_sorted, gs_sorted, blk_exp, blk_valid, w_up_gate, b_up_gate, w_down, b_down, tr):
    rows_total, w = xs_sorted.shape
    e, d, f2 = w_up_gate.shape
    f = f2 // 2
    n_tiles = rows_total // tr
    changed = jnp.concatenate([jnp.ones((1,), jnp.int32), (blk_exp[1:] != blk_exp[:-1]).astype(jnp.int32)])
    slot = (jnp.cumsum(changed) - 1) % 2
    tile_ids = jnp.arange(n_tiles, dtype=jnp.int32)
    later_first = (changed[None, :] == 1) & (tile_ids[None, :] > tile_ids[:, None])
    nxt_tile = jnp.min(jnp.where(later_first, tile_ids[None, :], n_tiles), axis=1)
    nxt_exp = jnp.where(nxt_tile < n_tiles, blk_exp[jnp.minimum(nxt_tile, n_tiles - 1)], -1).astype(jnp.int32)
    row_map = lambda i, *_: (i, 0)
    exp_map = lambda i, be, *_: (be[i], 0, 0)
    grid_spec = pltpu.PrefetchScalarGridSpec(
        num_scalar_prefetch=5,
        grid=(n_tiles,),
        in_specs=[
            pl.BlockSpec((tr, w), row_map),
            pl.BlockSpec((tr, LANES), row_map),
            pl.BlockSpec(memory_space=pl.ANY),
            pl.BlockSpec((None, 1, f2), exp_map),
            pl.BlockSpec(memory_space=pl.ANY),
            pl.BlockSpec((None, 1, d), exp_map),
        ],
        out_specs=pl.BlockSpec((tr, d // 2), row_map),
        scratch_shapes=[
            pltpu.VMEM((2, d, f2), F32), pltpu.VMEM((2, f, d), F32),
            pltpu.VMEM((d, f2), BF16), pltpu.VMEM((f, d), BF16),
            pltpu.SemaphoreType.DMA((2, 2)),
        ],
    )
    return pl.pallas_call(
        _experts_kernel,
        grid_spec=grid_spec,
        out_shape=jax.ShapeDtypeStruct((rows_total, d // 2), jnp.uint32),
        compiler_params=pltpu.CompilerParams(dimension_semantics=("arbitrary",), vmem_limit_bytes=VMEM_LIMIT_BYTES),
        name="experts",
    )(blk_exp, blk_valid, changed, slot.astype(jnp.int32), nxt_exp,
      xs_sorted, gs_sorted, w_up_gate, b_up_gate.reshape(e, 1, f2), w_down, b_down.reshape(e, 1, d))


def _combine_kernel(ast_ref, gdst_ref, nch_ref, h_ref, apos_ref, ys_ref, o_ref, loc_a, loc_b, sems, *, n_experts):
    i = pl.program_id(0)
    n_steps = pl.num_programs(0)
    tm = h_ref.shape[0]
    kloc = loc_a.shape[0]
    bufs = (loc_a, loc_b)

    def seg_copy(tile, slot, e, c):
        s0 = pl.multiple_of(ast_ref[tile * n_experts + e] + c * SEG_ALIGN, SEG_ALIGN)
        d0 = pl.multiple_of(gdst_ref[tile * n_experts + e] + c * SEG_ALIGN, SEG_ALIGN)
        return pltpu.make_async_copy(ys_ref.at[pl.ds(d0, SEG_ALIGN), :], bufs[slot].at[pl.ds(s0, SEG_ALIGN), :],
                                     sems.at[slot])

    def for_each_piece(tile, slot, fn):
        def per_expert(e, carry):
            def per_piece(c, carry):
                fn(seg_copy(tile, slot, e, c))
                return carry
            return lax.fori_loop(0, nch_ref[tile * n_experts + e], per_piece, carry)
        lax.fori_loop(0, n_experts, per_expert, 0)

    @pl.when(i == 0)
    def _():
        loc_a[...] = jnp.zeros_like(loc_a)
        loc_b[...] = jnp.zeros_like(loc_b)
        for_each_piece(0, 0, lambda cp: cp.start())

    def reduce_tile(slot):
        @pl.when(i + 1 < n_steps)
        def _():
            for_each_piece(i + 1, 1 - slot, lambda cp: cp.start())

        for_each_piece(i, slot, lambda cp: cp.wait())
        ap = apos_ref[...]
        a_cols = [ap[:, k:k + 1] for k in range(TOP_K)]
        m_chunks = []
        for c in range(kloc // PERM_CHUNK):
            r_iota = (lax.broadcasted_iota(jnp.int32, (tm, PERM_CHUNK), 1) + c * PERM_CHUNK).astype(F32)
            hit = jnp.zeros((tm, PERM_CHUNK), F32)
            for k in range(TOP_K):
                hit = jnp.where(r_iota == a_cols[k], 1.0, hit)
            m_chunks.append(hit.astype(BF16))
        first, second = _unpack_bf16_pairs(bufs[slot][...])
        rows = jnp.concatenate([first.astype(BF16), second.astype(BF16)], axis=1)
        o_ref[...] = h_ref[...] + jnp.dot(jnp.concatenate(m_chunks, axis=1), rows, preferred_element_type=F32)

    @pl.when(i % 2 == 0)
    def _():
        reduce_tile(0)

    @pl.when(i % 2 == 1)
    def _():
        reduce_tile(1)


def _combine(h, apos, ast, gdst, nch, ys, n_experts):
    n, d = h.shape
    tm = min(512, n)
    kloc = _local_rows(tm, n_experts)
    grid_spec = pltpu.PrefetchScalarGridSpec(
        num_scalar_prefetch=3,
        grid=(n // tm,),
        in_specs=[
            pl.BlockSpec((tm, d), lambda i, *_: (i, 0)),
            pl.BlockSpec((tm, LANES), lambda i, *_: (i, 0)),
            pl.BlockSpec(memory_space=pl.ANY),
        ],
        out_specs=pl.BlockSpec((tm, d), lambda i, *_: (i, 0)),
        scratch_shapes=[
            pltpu.VMEM((kloc, d // 2), jnp.uint32),
            pltpu.VMEM((kloc, d // 2), jnp.uint32),
            pltpu.SemaphoreType.DMA((2,)),
        ],
    )
    return pl.pallas_call(
        functools.partial(_combine_kernel, n_experts=n_experts),
        grid_spec=grid_spec,
        out_shape=jax.ShapeDtypeStruct((n, d), F32),
        compiler_params=pltpu.CompilerParams(dimension_semantics=("arbitrary",), vmem_limit_bytes=VMEM_LIMIT_BYTES),
        name="combine",
    )(ast, gdst, nch, h, apos, ys)


EXPERT_TILE = 512


def _layer(h2d, batch, seq, norm_mix, w_in, b_forget, q_norm_fox, k_norm_fox, q_norm_swa, k_norm_swa, sinks, rel_bias,
           w_out, norm_moe, w_router, b_router, w_up_gate, b_up_gate, w_down, b_down):
    n, d = h2d.shape
    n_experts = w_router.shape[1]
    qa, ka, fv, sq, sk, sv = _proj(h2d, norm_mix, w_in, b_forget, q_norm_fox, k_norm_fox, q_norm_swa, k_norm_swa, seq)
    fa = _fox(qa, ka, fv, batch, seq)
    sb = _swa(sq, sk, sv, sinks, rel_bias, batch, seq)
    h, hn, apos, rowst, tab = _outproj(h2d, fa, sb, w_out, norm_moe, w_router, b_router)

    tr = min(EXPERT_TILE, n)
    tab = tab.reshape(-1, 8, LANES)[:, :, :n_experts].astype(jnp.int32)
    base, cnt, ast = tab[:, 0], tab[:, 1], tab[:, 2]
    nch = (cnt + SEG_ALIGN - 1) // SEG_ALIGN
    counts_i = base[-1] + nch[-1] * SEG_ALIGN
    padded = (counts_i + tr - 1) // tr * tr
    pend = jnp.cumsum(padded)
    pstart = pend - padded
    gdst = pstart[None, :] + base
    n_tok_tiles = tab.shape[0]
    worst_rows = n * TOP_K + n_tok_tiles * n_experts * (SEG_ALIGN - 1) + n_experts * (tr - 1)
    n_tiles = -(-worst_rows // tr)
    rows_total = n_tiles * tr
    blk_start = jnp.arange(n_tiles, dtype=jnp.int32) * tr
    blk_exp = jnp.minimum(jnp.sum(pend[None, :] <= blk_start[:, None], axis=1), n_experts - 1).astype(jnp.int32)
    blk_valid = jnp.clip(pstart[blk_exp] + counts_i[blk_exp] - blk_start, 0, tr).astype(jnp.int32)
    ast, gdst, nch = (t.reshape(-1).astype(jnp.int32) for t in (ast, gdst, nch))

    xs_sorted, gs_sorted = _dispatch(hn, rowst, ast, gdst, nch, rows_total, n_experts)
    ys = _experts(xs_sorted, gs_sorted, blk_exp, blk_valid, w_up_gate, b_up_gate, w_down, b_down, tr)
    return _combine(h, apos, ast, gdst, nch, ys, n_experts)


def kernel(x, norm_mix, w_in, b_forget, q_norm_fox, k_norm_fox, q_norm_swa, k_norm_swa, sinks, rel_bias, w_out, norm_moe,
           w_router, b_router, w_up_gate, b_up_gate, w_down, b_down):
    batch, seq, d = x.shape
    h = x.reshape(batch * seq, d)
    for l in range(norm_mix.shape[0]):
        h = _layer(h, batch, seq, norm_mix[l], w_in[l], b_forget[l], q_norm_fox[l], k_norm_fox[l], q_norm_swa[l],
                   k_norm_swa[l], sinks[l], rel_bias, w_out[l], norm_moe[l], w_router[l], b_router[l], w_up_gate[l],
                   b_up_gate[l], w_down[l], b_down[l])
    return h.reshape(batch, seq, d)
```

```python
import functools
import math

import numpy as np
import jax
import jax.numpy as jnp
from jax import lax
from jax.experimental import pallas as pl
from jax.experimental.pallas import tpu as pltpu

HEAD_DIM = 64
FOX_HEADS = 8
SWA_HEADS = 8
SWA_KV_HEADS = 2
SWA_GROUP = SWA_HEADS // SWA_KV_HEADS
WINDOW = 128
N_BUCKETS = 32
MAX_DISTANCE = 128
TOP_K = 4
SWIGLU_LIMIT = 7.0
SWIGLU_ALPHA = 1.702
EPS = 1e-6

LANES = 128
VMEM_LIMIT_BYTES = 56 * 1024 * 1024
NEG_INF = float("-inf")
LOG2E = math.log2(math.e)

F32 = jnp.float32
BF16 = jnp.bfloat16


def _lane_iota(shape):
    return lax.broadcasted_iota(jnp.int32, shape, len(shape) - 1)


def _split3_bf16(v):
    hi = v.astype(BF16)
    r1 = v - hi.astype(F32)
    mid = r1.astype(BF16)
    lo = (r1 - mid.astype(F32)).astype(BF16)
    return hi, mid, lo


def _pack_bf16_pairs(v):
    w = v.shape[1] // 2
    u = pltpu.bitcast(v.astype(BF16).astype(F32), jnp.uint32)
    return u[:, :w] | (u[:, w:] >> 16)


def _unpack_bf16_pairs(u):
    first = pltpu.bitcast(u & jnp.uint32(0xFFFF0000), F32)
    second = pltpu.bitcast(u << 16, F32)
    return first, second


CUM_CHUNK = 128


def _proj_kernel(x_ref, g_ref, wqa_ref, wka_ref, wrest_ref, bf_ref, gqa_ref, gka_ref, gsq_ref, gsk_ref, tri_ref,
                 qa_ref, ka_ref, fv_ref, sq_ref, sk_ref, sv_ref, carry_ref, *, tiles_per_seq):
    i = pl.program_id(0)
    tm = x_ref.shape[0]
    nh = FOX_HEADS
    scale = HEAD_DIM ** -0.5

    @pl.when(i % tiles_per_seq == 0)
    def _():
        carry_ref[...] = jnp.zeros_like(carry_ref)

    x = x_ref[...]
    ms = jnp.mean(x * x, axis=-1, keepdims=True)
    hn = (x * lax.rsqrt(ms + EPS) * g_ref[...]).astype(BF16)

    qa = jnp.dot(hn, wqa_ref[...], preferred_element_type=F32)
    ka = jnp.dot(hn, wka_ref[...], preferred_element_type=F32)
    rest = jnp.dot(hn, wrest_ref[...], preferred_element_type=F32)

    lane_w = _lane_iota((tm, nh * LANES)) % LANES
    upper = lane_w >= HEAD_DIM

    fl = qa + bf_ref[...]
    ls = jnp.minimum(fl, 0.0) - jnp.log(1.0 + jnp.exp(-jnp.abs(fl)))
    ls = jnp.where(upper, ls, 0.0)
    tri = tri_ref[...]
    carry = carry_ref[0:1, :]
    c_chunks = []
    for r in range(tm // CUM_CHUNK):
        hi, mid, lo = _split3_bf16(ls[r * CUM_CHUNK:(r + 1) * CUM_CHUNK, :])
        cs = jnp.dot(tri, jnp.concatenate([hi, mid, lo], axis=0), preferred_element_type=F32) + carry
        carry = cs[CUM_CHUNK - 1:CUM_CHUNK, :]
        c_chunks.append(cs)
    carry_ref[0:1, :] = carry
    c = jnp.concatenate(c_chunks, axis=0) * LOG2E

    c1 = c.astype(BF16).astype(F32)
    c2 = (c - c1).astype(BF16).astype(F32)
    c3 = c - c1 - c2

    def head_norm(t, g_row):
        outs = []
        for h in range(t.shape[1] // LANES):
            th = t[:, h * LANES:(h + 1) * LANES]
            lo_mask = _lane_iota(th.shape) < HEAD_DIM
            ss = jnp.sum(jnp.where(lo_mask, th * th, 0.0), axis=-1, keepdims=True)
            outs.append(th * lax.rsqrt(ss * (1.0 / HEAD_DIM) + EPS))
        return jnp.concatenate(outs, axis=1) * g_row

    qn = head_norm(qa, gqa_ref[...]) * (scale * LOG2E)
    kn = head_norm(ka, gka_ref[...])

    one = jnp.float32(1.0)
    zero = jnp.float32(0.0)
    q_aug = jnp.where(lane_w == 64, c1, jnp.where(lane_w == 65, c2, jnp.where(lane_w == 66, c3,
                      jnp.where(lane_w < 70, one, zero))))
    k_aug = jnp.where(lane_w < 67, one, jnp.where(lane_w == 67, -c1, jnp.where(lane_w == 68, -c2,
                      jnp.where(lane_w == 69, -c3, zero))))
    qa_ref[...] = jnp.where(upper, q_aug, qn).astype(BF16)
    ka_ref[...] = jnp.where(upper, k_aug, kn).astype(BF16)

    nfv = FOX_HEADS * HEAD_DIM
    nsq = SWA_HEADS * HEAD_DIM
    nsk = SWA_KV_HEADS * HEAD_DIM
    fv_ref[...] = rest[:, :nfv].astype(BF16)

    def half_norm(t, g_row):
        outs = []
        for j in range(t.shape[1] // LANES):
            tj = t[:, j * LANES:(j + 1) * LANES]
            lo_mask = _lane_iota(tj.shape) < HEAD_DIM
            sq = tj * tj
            s_lo = jnp.sum(jnp.where(lo_mask, sq, 0.0), axis=-1, keepdims=True)
            s_hi = jnp.sum(jnp.where(lo_mask, 0.0, sq), axis=-1, keepdims=True)
            r = jnp.where(lo_mask, lax.rsqrt(s_lo * (1.0 / HEAD_DIM) + EPS), lax.rsqrt(s_hi * (1.0 / HEAD_DIM) + EPS))
            outs.append(tj * r)
        return jnp.concatenate(outs, axis=1) * g_row

    sq_ref[...] = (half_norm(rest[:, nfv:nfv + nsq], gsq_ref[...]) * scale).astype(BF16)
    sk_ref[...] = half_norm(rest[:, nfv + nsq:nfv + nsq + nsk], gsk_ref[...]).astype(BF16)
    sv_ref[...] = rest[:, nfv + nsq + nsk:].astype(BF16)


def _proj(x2d, norm_mix, w_in, b_forget, q_norm_fox, k_norm_fox, q_norm_swa, k_norm_swa, seq):
    n, d = x2d.shape
    nh, dh = FOX_HEADS, HEAD_DIM
    tm = min(512, seq)
    assert seq % tm == 0 and tm % CUM_CHUNK == 0
    points = np.cumsum([nh * dh, nh * dh, nh * dh, nh, SWA_HEADS * dh, SWA_KV_HEADS * dh, SWA_KV_HEADS * dh])
    wfq, wfk, wfv, wff, wsq, wsk, wsv = jnp.split(w_in, [int(p) for p in points[:-1]], axis=1)

    wfq = wfq.reshape(d, nh, dh)
    wff_rep = jnp.broadcast_to(wff[:, :, None], (d, nh, dh))
    wqa = jnp.concatenate([wfq, wff_rep], axis=2).reshape(d, nh * LANES).astype(BF16)
    wka = jnp.concatenate([wfk.reshape(d, nh, dh), jnp.zeros((d, nh, dh), w_in.dtype)], axis=2)
    wka = wka.reshape(d, nh * LANES).astype(BF16)
    wsq = wsq.reshape(d, SWA_KV_HEADS, SWA_GROUP, dh).transpose(0, 2, 1, 3).reshape(d, SWA_HEADS * dh)
    wrest = jnp.concatenate([wfv, wsq, wsk, wsv], axis=1).astype(BF16)

    zeros_h = jnp.zeros((nh, dh), F32)
    bf_row = jnp.concatenate([zeros_h, jnp.broadcast_to(b_forget[:, None], (nh, dh))], axis=1).reshape(1, nh * LANES)
    gqa = jnp.concatenate([jnp.broadcast_to(q_norm_fox[None], (nh, dh)), zeros_h], axis=1).reshape(1, nh * LANES)
    gka = jnp.concatenate([jnp.broadcast_to(k_norm_fox[None], (nh, dh)), zeros_h], axis=1).reshape(1, nh * LANES)
    gsq = jnp.tile(q_norm_swa, SWA_HEADS).reshape(1, SWA_HEADS * dh)
    gsk = jnp.tile(k_norm_swa, SWA_KV_HEADS).reshape(1, SWA_KV_HEADS * dh)
    tri = np.tril(np.ones((CUM_CHUNK, CUM_CHUNK), np.float32))
    tri3 = jnp.asarray(np.concatenate([tri, tri, tri], axis=1), BF16)

    nrest = wrest.shape[1]
    const = lambda i: (0, 0)
    row = lambda i: (i, 0)
    out_shapes = (
        jax.ShapeDtypeStruct((n, nh * LANES), BF16),
        jax.ShapeDtypeStruct((n, nh * LANES), BF16),
        jax.ShapeDtypeStruct((n, nh * dh), BF16),
        jax.ShapeDtypeStruct((n, SWA_HEADS * dh), BF16),
        jax.ShapeDtypeStruct((n, SWA_KV_HEADS * dh), BF16),
        jax.ShapeDtypeStruct((n, SWA_KV_HEADS * dh), BF16),
    )
    return pl.pallas_call(
        functools.partial(_proj_kernel, tiles_per_seq=seq // tm),
        grid=(n // tm,),
        in_specs=[
            pl.BlockSpec((tm, d), row),
            pl.BlockSpec((1, d), const),
            pl.BlockSpec((d, nh * LANES), const),
            pl.BlockSpec((d, nh * LANES), const),
            pl.BlockSpec((d, nrest), const),
            pl.BlockSpec((1, nh * LANES), const),
            pl.BlockSpec((1, nh * LANES), const),
            pl.BlockSpec((1, nh * LANES), const),
            pl.BlockSpec((1, SWA_HEADS * dh), const),
            pl.BlockSpec((1, SWA_KV_HEADS * dh), const),
            pl.BlockSpec((CUM_CHUNK, 3 * CUM_CHUNK), const),
        ],
        out_specs=[
            pl.BlockSpec((tm, nh * LANES), row),
            pl.BlockSpec((tm, nh * LANES), row),
            pl.BlockSpec((tm, nh * dh), row),
            pl.BlockSpec((tm, SWA_HEADS * dh), row),
            pl.BlockSpec((tm, SWA_KV_HEADS * dh), row),
            pl.BlockSpec((tm, SWA_KV_HEADS * dh), row),
        ],
        out_shape=out_shapes,
        scratch_shapes=[pltpu.VMEM((8, nh * LANES), F32)],
        compiler_params=pltpu.CompilerParams(dimension_semantics=("arbitrary",), vmem_limit_bytes=VMEM_LIMIT_BYTES),
        name="proj",
    )(x2d, norm_mix.reshape(1, d), wqa, wka, wrest, bf_row, gqa, gka, gsq, gsk, tri3)


FOX_BLOCK = 512


def _fox_kernel(qa_ref, ka_ref, v_ref, o_ref, s0_sc, s1_sc, m_sc, l_sc, acc_sc, *, blk):
    i = pl.program_id(2)
    nt = (((1,), (1,)), ((), ()))
    row = lax.broadcasted_iota(jnp.int32, (blk, blk), 0)
    col = lax.broadcasted_iota(jnp.int32, (blk, blk), 1)
    causal = col <= row
    s_bufs = (s0_sc, s1_sc)

    def lane_tiles(s):
        return [s[:, c * LANES:(c + 1) * LANES] for c in range(s.shape[1] // LANES)]

    def row_max(tiles):
        m = jnp.max(functools.reduce(jnp.maximum, tiles), axis=-1, keepdims=True)
        return jnp.broadcast_to(m, (blk, LANES))

    def row_sum(tiles):
        t = jnp.sum(functools.reduce(jnp.add, tiles), axis=-1, keepdims=True)
        return jnp.broadcast_to(t, (blk, LANES))

    def produce_scores(j, slot):
        start = pl.multiple_of(j * blk, blk)
        for hh in range(2):
            q = qa_ref[:, hh * LANES:(hh + 1) * LANES]
            k = ka_ref[pl.ds(start, blk), hh * LANES:(hh + 1) * LANES]
            s_bufs[slot][hh] = lax.dot_general(q, k, nt, preferred_element_type=F32)

    def consume_scores(j, slot, masked):
        v = v_ref[pl.ds(pl.multiple_of(j * blk, blk), blk), :]
        for hh in range(2):
            s = s_bufs[slot][hh]
            if masked:
                s = jnp.where(causal, s, NEG_INF)
            tiles = lane_tiles(s)
            m = m_sc[hh]
            m_new = jnp.maximum(m, row_max(tiles))
            alpha = jnp.exp2(m - m_new)
            p_tiles = [jnp.exp2(t - m_new) for t in tiles]
            p = jnp.concatenate(p_tiles, axis=1).astype(BF16)
            m_sc[hh] = m_new
            l_sc[hh] = alpha * l_sc[hh] + row_sum(p_tiles)
            acc_sc[hh] = alpha * acc_sc[hh] + jnp.dot(p, v, preferred_element_type=F32)

    m_sc[...] = jnp.full(m_sc.shape, NEG_INF, F32)
    l_sc[...] = jnp.zeros(l_sc.shape, F32)
    acc_sc[...] = jnp.zeros(acc_sc.shape, F32)
    produce_scores(0, 0)

    def pair(jj, c):
        j = 2 * jj
        produce_scores(j + 1, 1)
        consume_scores(j, 0, masked=False)
        produce_scores(j + 2, 0)
        consume_scores(j + 1, 1, masked=False)
        return c

    lax.fori_loop(0, i // 2, pair, 0)

    @pl.when(i % 2 == 0)
    def _():
        consume_scores(i, 0, masked=True)

    @pl.when(i % 2 == 1)
    def _():
        produce_scores(i, 1)
        consume_scores(i - 1, 0, masked=False)
        consume_scores(i, 1, masked=True)

    out0 = acc_sc[0] / l_sc[0]
    out1 = acc_sc[1] / l_sc[1]
    lane = _lane_iota(out0.shape)
    o_ref[...] = jnp.where(lane < HEAD_DIM, out0, out1).astype(o_ref.dtype)


def _fox(qa, ka, fv, batch, seq):
    nh, dh = FOX_HEADS, HEAD_DIM
    blk = min(FOX_BLOCK, seq)
    assert seq % blk == 0
    qa3 = qa.reshape(batch, seq, nh * LANES)
    ka3 = ka.reshape(batch, seq, nh * LANES)
    fv3 = fv.reshape(batch, seq, nh * dh)
    out = pl.pallas_call(
        functools.partial(_fox_kernel, blk=blk),
        grid=(batch, nh // 2, seq // blk),
        in_specs=[
            pl.BlockSpec((None, blk, 2 * LANES), lambda b, j, i: (b, i, j)),
            pl.BlockSpec((None, seq, 2 * LANES), lambda b, j, i: (b, 0, j)),
            pl.BlockSpec((None, seq, LANES), lambda b, j, i: (b, 0, j)),
        ],
        out_specs=pl.BlockSpec((None, blk, LANES), lambda b, j, i: (b, i, j)),
        out_shape=jax.ShapeDtypeStruct((batch, seq, nh * dh), BF16),
        scratch_shapes=[
            pltpu.VMEM((2, blk, blk), F32),
            pltpu.VMEM((2, blk, blk), F32),
            pltpu.VMEM((2, blk, LANES), F32),
            pltpu.VMEM((2, blk, LANES), F32),
            pltpu.VMEM((2, blk, LANES), F32),
        ],
        compiler_params=pltpu.CompilerParams(dimension_semantics=("arbitrary", "arbitrary", "arbitrary"),
                                             vmem_limit_bytes=VMEM_LIMIT_BYTES),
        name="fox",
    )(qa3, ka3, fv3)
    return out.reshape(batch * seq, nh * dh)


def _t5_bucket_table():
    qi = np.arange(WINDOW)[:, None]
    kj = np.arange(2 * WINDOW)[None, :]
    dist = qi + WINDOW - kj
    n = np.maximum(dist, 0)
    max_exact = N_BUCKETS // 2
    nf = np.maximum(n, 1).astype(np.float64)
    large = max_exact + (np.log(nf / max_exact) / math.log(MAX_DISTANCE / max_exact) * (N_BUCKETS - max_exact)).astype(np.int64)
    large = np.minimum(large, N_BUCKETS - 1)
    bucket = np.where(n < max_exact, n, large)
    valid = (dist >= 0) & (dist < WINDOW)
    return np.where(valid, bucket, -1).astype(np.int32)


def _swa_kernel(rb_ref, sinks_ref, bucket_ref, sq_ref, sk_ref, sv_ref, skp_ref, svp_ref, o_ref, bias_ref, *, blocks_per_tile):
    b = pl.program_id(0)
    i = pl.program_id(1)
    w = WINDOW
    g = SWA_GROUP

    @pl.when((b == 0) & (i == 0))
    def _():
        bucket = bucket_ref[...]
        for h in range(SWA_HEADS):
            acc = jnp.full((w, 2 * w), NEG_INF, F32)
            for bk in range(N_BUCKETS):
                acc = jnp.where(bucket == bk, rb_ref[bk, h], acc)
            kv, gi = divmod(h, g)
            bias_ref[kv, gi * w:(gi + 1) * w, :] = acc

    nt = (((1,), (1,)), ((), ()))
    lane = _lane_iota((w, LANES))
    col = _lane_iota((g * w, 2 * w))
    for cb in range(blocks_per_tile):
        rows = slice(cb * w, (cb + 1) * w)
        if cb == 0:
            kprev, vprev = skp_ref[...], svp_ref[...]
        else:
            kprev, vprev = sk_ref[(cb - 1) * w:cb * w, :], sv_ref[(cb - 1) * w:cb * w, :]
        kcat = jnp.concatenate([kprev, sk_ref[rows, :]], axis=0)
        vcat = jnp.concatenate([vprev, sv_ref[rows, :]], axis=0)
        outs = []
        for kv in range(SWA_KV_HEADS):
            keep = (lane < HEAD_DIM) if kv == 0 else (lane >= HEAD_DIM)
            qs = [jnp.where(keep, sq_ref[rows, gi * LANES:(gi + 1) * LANES], jnp.zeros((), BF16)) for gi in range(g)]
            qstack = jnp.concatenate(qs, axis=0)
            s = lax.dot_general(qstack, kcat, nt, preferred_element_type=F32) + bias_ref[kv]
            if cb == 0:
                s = s + jnp.where(col < w, jnp.where(i == 0, NEG_INF, 0.0), 0.0)
            sink = jnp.concatenate([jnp.full((w, 1), sinks_ref[kv * g + gi], F32) for gi in range(g)], axis=0)
            m = jnp.maximum(jnp.max(s, axis=-1, keepdims=True), sink)
            p = jnp.exp(s - m)
            denom = jnp.sum(p, axis=-1, keepdims=True) + jnp.exp(sink - m)
            outs.append(jnp.dot(p.astype(BF16), vcat, preferred_element_type=F32) / denom)
        for gi in range(g):
            o_ref[rows, gi * LANES:(gi + 1) * LANES] = jnp.where(
                lane < HEAD_DIM, outs[0][gi * w:(gi + 1) * w, :], outs[1][gi * w:(gi + 1) * w, :]).astype(o_ref.dtype)


def _swa(sq, sk, sv, sinks, rel_bias, batch, seq):
    w = WINDOW
    dh = HEAD_DIM
    blocks_per_tile = min(4, seq // w)
    t = blocks_per_tile * w
    assert seq % t == 0
    sq3 = sq.reshape(batch, seq, SWA_HEADS * dh)
    sk3 = sk.reshape(batch, seq, SWA_KV_HEADS * dh)
    sv3 = sv.reshape(batch, seq, SWA_KV_HEADS * dh)
    bucket = jnp.asarray(_t5_bucket_table())
    cur = lambda b, i: (b, i, 0)
    prev = lambda b, i: (b, jnp.maximum(i * blocks_per_tile - 1, 0), 0)
    smem = pl.BlockSpec(memory_space=pltpu.SMEM)
    out = pl.pallas_call(
        functools.partial(_swa_kernel, blocks_per_tile=blocks_per_tile),
        grid=(batch, seq // t),
        in_specs=[
            smem, smem,
            pl.BlockSpec((w, 2 * w), lambda b, i: (0, 0)),
            pl.BlockSpec((None, t, SWA_HEADS * dh), cur),
            pl.BlockSpec((None, t, SWA_KV_HEADS * dh), cur),
            pl.BlockSpec((None, t, SWA_KV_HEADS * dh), cur),
            pl.BlockSpec((None, w, SWA_KV_HEADS * dh), prev),
            pl.BlockSpec((None, w, SWA_KV_HEADS * dh), prev),
        ],
        out_specs=pl.BlockSpec((None, t, SWA_HEADS * dh), cur),
        out_shape=jax.ShapeDtypeStruct((batch, seq, SWA_HEADS * dh), BF16),
        scratch_shapes=[pltpu.VMEM((SWA_KV_HEADS, SWA_GROUP * w, 2 * w), F32)],
        compiler_params=pltpu.CompilerParams(dimension_semantics=("arbitrary", "arbitrary"),
                                             vmem_limit_bytes=VMEM_LIMIT_BYTES),
        name="swa",
    )(rel_bias, sinks, bucket, sq3, sk3, sv3, sk3, sv3)
    return out.reshape(batch * seq, SWA_HEADS * dh)


SEG_ALIGN = 8
SEL_ROWS = 32
PERM_CHUNK = 256
PIECE_UNROLL = 8


def _local_rows(tm, n_experts):
    return -(-(tm * TOP_K + n_experts * SEG_ALIGN) // PERM_CHUNK) * PERM_CHUNK


def _outproj_kernel(x_ref, fa_ref, sb_ref, wa_ref, wb_ref, g_ref, wrh_ref, wrl_ref, br_ref, lstrict_ref, ustrict_ref,
                    sel_ref, h_ref, hn_ref, apos_ref, rowst_ref, tab_ref, carry_ref, *, n_experts):
    i = pl.program_id(0)
    tm = x_ref.shape[0]

    @pl.when(i == 0)
    def _():
        carry_ref[...] = jnp.zeros_like(carry_ref)

    h = (x_ref[...] + jnp.dot(fa_ref[...], wa_ref[...], preferred_element_type=F32)
         + jnp.dot(sb_ref[...], wb_ref[...], preferred_element_type=F32))
    h_ref[...] = h
    ms = jnp.mean(h * h, axis=-1, keepdims=True)
    hn = h * lax.rsqrt(ms + EPS) * g_ref[...]
    hn_ref[...] = hn.astype(BF16)

    hn_hi = hn.astype(BF16)
    hn_lo = (hn - hn_hi.astype(F32)).astype(BF16)
    wrh = wrh_ref[...]
    logits = (jnp.dot(hn_hi, wrh, preferred_element_type=F32) + jnp.dot(hn_lo, wrh, preferred_element_type=F32)
              + jnp.dot(hn_hi, wrl_ref[...], preferred_element_type=F32) + br_ref[...])
    lane = _lane_iota((tm, LANES))
    lane_f = lane.astype(F32)
    l = jnp.where(lane < n_experts, logits, NEG_INF)
    vals, idxs = [], []
    member = jnp.zeros((tm, LANES), F32)
    for _ in range(TOP_K):
        mk = jnp.max(l, axis=-1, keepdims=True)
        ik = jnp.min(jnp.where(l == mk, lane_f, float(LANES)), axis=-1, keepdims=True)
        sel = lane_f == ik
        l = jnp.where(sel, NEG_INF, l)
        member = jnp.where(sel, 1.0, member)
        vals.append(mk)
        idxs.append(ik)
    es = [jnp.exp(v - vals[0]) for v in vals]
    tot = es[0] + es[1] + es[2] + es[3]
    gates = [e / tot for e in es]

    lrank = jnp.dot(lstrict_ref[...], member.astype(BF16), preferred_element_type=F32)
    cnt = lrank[tm - 1:tm, :] + member[tm - 1:tm, :]
    cnt_al = jnp.floor((cnt + (SEG_ALIGN - 1)) * (1.0 / SEG_ALIGN)) * SEG_ALIGN
    base = carry_ref[0:1, :]
    carry_ref[0:1, :] = base + cnt_al
    astart = jnp.dot(jnp.broadcast_to(cnt_al, (8, LANES)).astype(BF16), ustrict_ref[...],
                     preferred_element_type=F32)[0:1, :]
    local = astart + lrank
    apos = [jnp.sum(jnp.where(lane_f == ik, local, 0.0), axis=-1, keepdims=True) for ik in idxs]

    tab_ref[...] = jnp.concatenate([base, cnt, astart, jnp.zeros((5, LANES), F32)], axis=0)
    ap = jnp.zeros((tm, LANES), F32)
    for k in range(TOP_K):
        ap = jnp.where(lane == k, apos[k], ap)
    apos_ref[...] = ap

    cols = jnp.zeros((tm, LANES), F32)
    for k in range(TOP_K):
        hi = jnp.floor(apos[k] * (1.0 / 64.0))
        cols = jnp.where(lane == 2 * k, hi, cols)
        cols = jnp.where(lane == 2 * k + 1, apos[k] - 64.0 * hi, cols)
        g1 = gates[k].astype(BF16).astype(F32)
        g2 = (gates[k] - g1).astype(BF16).astype(F32)
        cols = jnp.where(lane == 2 * TOP_K + 3 * k, g1, cols)
        cols = jnp.where(lane == 2 * TOP_K + 3 * k + 1, g2, cols)
        cols = jnp.where(lane == 2 * TOP_K + 3 * k + 2, gates[k] - g1 - g2, cols)
    rowst_ref[...] = lax.dot_general(sel_ref[...], cols.astype(BF16), (((1,), (1,)), ((), ())),
                                     preferred_element_type=F32)


def _outproj(x2d, fa, sb, w_out, norm_moe, w_router, b_router):
    n, d = x2d.shape
    n_experts = w_router.shape[1]
    assert n_experts <= LANES
    tm = min(512, n)
    assert n % tm == 0
    nfa = FOX_HEADS * HEAD_DIM
    wa = w_out[:nfa].astype(BF16)
    wb = w_out[nfa:].reshape(SWA_KV_HEADS, SWA_GROUP, HEAD_DIM, d).transpose(1, 0, 2, 3).reshape(SWA_HEADS * HEAD_DIM, d)
    wb = wb.astype(BF16)
    wr = jnp.zeros((d, LANES), F32).at[:, :n_experts].set(w_router)
    wr_hi = wr.astype(BF16)
    wr_lo = (wr - wr_hi.astype(F32)).astype(BF16)
    br = jnp.zeros((1, LANES), F32).at[0, :n_experts].set(b_router)
    lstrict = jnp.asarray(np.tril(np.ones((tm, tm), np.float32), -1), BF16)
    ustrict = jnp.asarray(np.triu(np.ones((LANES, LANES), np.float32), 1), BF16)
    sel = jnp.asarray(np.eye(SEL_ROWS, LANES, dtype=np.float32), BF16)
    assert 2 * TOP_K + 3 * TOP_K <= SEL_ROWS and _local_rows(tm, n_experts) <= 64 * 256
    const = lambda i: (0, 0)
    row = lambda i: (i, 0)
    return pl.pallas_call(
        functools.partial(_outproj_kernel, n_experts=n_experts),
        grid=(n // tm,),
        in_specs=[
            pl.BlockSpec((tm, d), row),
            pl.BlockSpec((tm, nfa), row),
            pl.BlockSpec((tm, SWA_HEADS * HEAD_DIM), row),
            pl.BlockSpec((nfa, d), const),
            pl.BlockSpec((SWA_HEADS * HEAD_DIM, d), const),
            pl.BlockSpec((1, d), const),
            pl.BlockSpec((d, LANES), const),
            pl.BlockSpec((d, LANES), const),
            pl.BlockSpec((1, LANES), const),
            pl.BlockSpec((tm, tm), const),
            pl.BlockSpec((LANES, LANES), const),
            pl.BlockSpec((SEL_ROWS, LANES), const),
        ],
        out_specs=[
            pl.BlockSpec((tm, d), row),
            pl.BlockSpec((tm, d), row),
            pl.BlockSpec((tm, LANES), row),
            pl.BlockSpec((SEL_ROWS, tm), lambda i: (0, i)),
            pl.BlockSpec((8, LANES), row),
        ],
        out_shape=(
            jax.ShapeDtypeStruct((n, d), F32),
            jax.ShapeDtypeStruct((n, d), BF16),
            jax.ShapeDtypeStruct((n, LANES), F32),
            jax.ShapeDtypeStruct((SEL_ROWS, n), F32),
            jax.ShapeDtypeStruct((n // tm * 8, LANES), F32),
        ),
        scratch_shapes=[pltpu.VMEM((8, LANES), F32)],
        compiler_params=pltpu.CompilerParams(dimension_semantics=("arbitrary",), vmem_limit_bytes=VMEM_LIMIT_BYTES),
        name="outproj",
    )(x2d, fa, sb, wa, wb, norm_moe.reshape(1, d), wr_hi, wr_lo, br, lstrict, ustrict, sel)


def _piece_copies_start(make_copy, n_pieces):
    assert n_pieces % PIECE_UNROLL == 0

    def body(g, carry):
        for u in range(PIECE_UNROLL):
            make_copy(g * PIECE_UNROLL + u).start(priority=u % 2)
        return carry
    lax.fori_loop(0, n_pieces // PIECE_UNROLL, body, 0)


def _dispatch_kernel(dst_ref, hn_ref, rowst_ref, xs_out, perm, loc, sem):
    tm = hn_ref.shape[0]
    kloc = loc.shape[0]
    w = hn_ref.shape[1] // 2

    rows = rowst_ref[...]
    a_rows = [64.0 * rows[2 * k:2 * k + 1, :] + rows[2 * k + 1:2 * k + 2, :] for k in range(TOP_K)]
    g_rows = [rows[2 * TOP_K + 3 * k:2 * TOP_K + 3 * k + 1, :] + rows[2 * TOP_K + 3 * k + 1:2 * TOP_K + 3 * k + 2, :]
              + rows[2 * TOP_K + 3 * k + 2:2 * TOP_K + 3 * k + 3, :] for k in range(TOP_K)]

    for c in range(kloc // PERM_CHUNK):
        r_iota = (lax.broadcasted_iota(jnp.int32, (PERM_CHUNK, tm), 0) + c * PERM_CHUNK).astype(F32)
        hit = jnp.zeros((PERM_CHUNK, tm), F32)
        gw = jnp.zeros((PERM_CHUNK, tm), F32)
        for k in range(TOP_K):
            eq = r_iota == a_rows[k]
            hit = jnp.where(eq, 1.0, hit)
            gw = jnp.where(eq, g_rows[k], gw)
        perm[c * PERM_CHUNK:(c + 1) * PERM_CHUNK, :] = hit.astype(BF16)
        gsum = functools.reduce(jnp.add, [gw[:, j * LANES:(j + 1) * LANES] for j in range(tm // LANES)])
        gate = jnp.broadcast_to(jnp.sum(gsum, axis=-1, keepdims=True), (PERM_CHUNK, LANES))
        loc[c * PERM_CHUNK:(c + 1) * PERM_CHUNK, w:] = pltpu.bitcast(gate, jnp.uint32)

    loc[:, :w] = _pack_bf16_pairs(jnp.dot(perm[...], hn_ref[...], preferred_element_type=F32))

    def piece_copy(p):
        s0 = pl.multiple_of(p * SEG_ALIGN, SEG_ALIGN)
        d0 = pl.multiple_of(dst_ref[0, 0, p], SEG_ALIGN)
        return pltpu.make_async_copy(loc.at[pl.ds(s0, SEG_ALIGN), :], xs_out.at[pl.ds(d0, SEG_ALIGN), :], sem)

    _piece_copies_start(piece_copy, kloc // SEG_ALIGN)
    pltpu.make_async_copy(loc, xs_out.at[pl.ds(0, kloc), :], sem).wait()


def _dispatch(hn, rowst, piece_dst, rows_total, n_experts):
    n, d = hn.shape
    tm = min(512, n)
    kloc = _local_rows(tm, n_experts)
    n_pieces = kloc // SEG_ALIGN
    return pl.pallas_call(
        _dispatch_kernel,
        grid=(n // tm,),
        in_specs=[
            pl.BlockSpec((1, 1, n_pieces), lambda i: (i, 0, 0), memory_space=pltpu.SMEM),
            pl.BlockSpec((tm, d), lambda i: (i, 0)),
            pl.BlockSpec((SEL_ROWS, tm), lambda i: (0, i)),
        ],
        out_specs=pl.BlockSpec(memory_space=pl.ANY),
        out_shape=jax.ShapeDtypeStruct((rows_total, d // 2 + LANES), jnp.uint32),
        scratch_shapes=[
            pltpu.VMEM((kloc, tm), BF16),
            pltpu.VMEM((kloc, d // 2 + LANES), jnp.uint32),
            pltpu.SemaphoreType.DMA(()),
        ],
        compiler_params=pltpu.CompilerParams(dimension_semantics=("arbitrary",), vmem_limit_bytes=VMEM_LIMIT_BYTES,
                                             has_side_effects=True),
        name="dispatch",
    )(piece_dst, hn, rowst)


def _experts_kernel(be_ref, bv_ref, first_ref, slot_ref, nxt_ref, x_ref, wug_hbm, bug_ref, wd_hbm, bd_ref, y_ref,
                    wug_f32, wd_f32, wug_bf, wd_bf, sems):
    i = pl.program_id(0)
    f = wd_bf.shape[0]

    def weight_copies(expert, slot):
        return (pltpu.make_async_copy(wug_hbm.at[expert], wug_f32.at[slot], sems.at[slot, 0]),
                pltpu.make_async_copy(wd_hbm.at[expert], wd_f32.at[slot], sems.at[slot, 1]))

    @pl.when(i == 0)
    def _():
        for cp in weight_copies(be_ref[0], 0):
            cp.start()

    @pl.when(first_ref[i] == 1)
    def _():
        slot = slot_ref[i]
        for cp in weight_copies(be_ref[i], slot):
            cp.wait()
        wug_bf[...] = wug_f32[slot].astype(BF16)
        wd_bf[...] = wd_f32[slot].astype(BF16)

        @pl.when(nxt_ref[i] >= 0)
        def _():
            for cp in weight_copies(nxt_ref[i], 1 - slot):
                cp.start()

    @pl.when(bv_ref[i] > 0)
    def _():
        w = y_ref.shape[1]
        xw = x_ref[...]
        row = lax.broadcasted_iota(jnp.int32, xw.shape, 0)
        xw = jnp.where(row < bv_ref[i], xw, jnp.uint32(0))
        first, second = _unpack_bf16_pairs(xw[:, :w])
        x = jnp.concatenate([first.astype(BF16), second.astype(BF16)], axis=1)
        gu = jnp.dot(x, wug_bf[...], preferred_element_type=F32) + bug_ref[...]
        gate = jnp.minimum(gu[:, :f], SWIGLU_LIMIT)
        up = jnp.clip(gu[:, f:], -SWIGLU_LIMIT, SWIGLU_LIMIT)
        act = (up + 1.0) * (gate * (1.0 / (1.0 + jnp.exp(-SWIGLU_ALPHA * gate))))
        y = jnp.dot(act.astype(BF16), wd_bf[...], preferred_element_type=F32) + bd_ref[...]
        gate_row = pltpu.bitcast(xw[:, w:], F32)
        y = y * jnp.concatenate([gate_row] * (y.shape[1] // LANES), axis=1)
        y_ref[...] = _pack_bf16_pairs(y)

    @pl.when(bv_ref[i] <= 0)
    def _():
        y_ref[...] = jnp.zeros_like(y_ref)


def _experts(xs_sorted, blk_exp, blk_valid, w_up_gate, b_up_gate, w_down, b_down, tr):
    rows_total, w = xs_sorted.shape
    e, d, f2 = w_up_gate.shape
    f = f2 // 2
    n_tiles = rows_total // tr
    changed = jnp.concatenate([jnp.ones((1,), jnp.int32), (blk_exp[1:] != blk_exp[:-1]).astype(jnp.int32)])
    slot = (jnp.cumsum(changed) - 1) % 2
    tile_ids = jnp.arange(n_tiles, dtype=jnp.int32)
    later_first = (changed[None, :] == 1) & (tile_ids[None, :] > tile_ids[:, None])
    nxt_tile = jnp.min(jnp.where(later_first, tile_ids[None, :], n_tiles), axis=1)
    nxt_exp = jnp.where(nxt_tile < n_tiles, blk_exp[jnp.minimum(nxt_tile, n_tiles - 1)], -1).astype(jnp.int32)
    row_map = lambda i, *_: (i, 0)
    exp_map = lambda i, be, *_: (be[i], 0, 0)
    grid_spec = pltpu.PrefetchScalarGridSpec(
        num_scalar_prefetch=5,
        grid=(n_tiles,),
        in_specs=[
            pl.BlockSpec((tr, w), row_map),
            pl.BlockSpec(memory_space=pl.ANY),
            pl.BlockSpec((None, 1, f2), exp_map),
            pl.BlockSpec(memory_space=pl.ANY),
            pl.BlockSpec((None, 1, d), exp_map),
        ],
        out_specs=pl.BlockSpec((tr, d // 2), row_map),
        scratch_shapes=[
            pltpu.VMEM((2, d, f2), F32), pltpu.VMEM((2, f, d), F32),
            pltpu.VMEM((d, f2), BF16), pltpu.VMEM((f, d), BF16),
            pltpu.SemaphoreType.DMA((2, 2)),
        ],
    )
    return pl.pallas_call(
        _experts_kernel,
        grid_spec=grid_spec,
        out_shape=jax.ShapeDtypeStruct((rows_total, d // 2), jnp.uint32),
        compiler_params=pltpu.CompilerParams(dimension_semantics=("arbitrary",), vmem_limit_bytes=VMEM_LIMIT_BYTES),
        name="experts",
    )(blk_exp, blk_valid, changed, slot.astype(jnp.int32), nxt_exp,
      xs_sorted, w_up_gate, b_up_gate.reshape(e, 1, f2), w_down, b_down.reshape(e, 1, d))


def _combine_kernel(src_ref, srcn_ref, h_ref, apos_ref, ys_ref, o_ref, loc_a, loc_b, sems):
    i = pl.program_id(0)
    n_steps = pl.num_programs(0)
    tm = h_ref.shape[0]
    kloc = loc_a.shape[0]
    bufs = (loc_a, loc_b)

    def fetch(table_ref, slot):
        def piece_copy(p):
            s0 = pl.multiple_of(table_ref[0, 0, p], SEG_ALIGN)
            d0 = pl.multiple_of(p * SEG_ALIGN, SEG_ALIGN)
            return pltpu.make_async_copy(ys_ref.at[pl.ds(s0, SEG_ALIGN), :], bufs[slot].at[pl.ds(d0, SEG_ALIGN), :],
                                         sems.at[slot])
        _piece_copies_start(piece_copy, kloc // SEG_ALIGN)

    @pl.when(i == 0)
    def _():
        fetch(src_ref, 0)

    def reduce_tile(slot):
        @pl.when(i + 1 < n_steps)
        def _():
            fetch(srcn_ref, 1 - slot)

        pltpu.make_async_copy(ys_ref.at[pl.ds(0, kloc), :], bufs[slot], sems.at[slot]).wait()
        ap = apos_ref[...]
        a_cols = [ap[:, k:k + 1] for k in range(TOP_K)]
        m_chunks = []
        for c in range(kloc // PERM_CHUNK):
            r_iota = (lax.broadcasted_iota(jnp.int32, (tm, PERM_CHUNK), 1) + c * PERM_CHUNK).astype(F32)
            hit = jnp.zeros((tm, PERM_CHUNK), F32)
            for k in range(TOP_K):
                hit = jnp.where(r_iota == a_cols[k], 1.0, hit)
            m_chunks.append(hit.astype(BF16))
        first, second = _unpack_bf16_pairs(bufs[slot][...])
        rows = jnp.concatenate([first.astype(BF16), second.astype(BF16)], axis=1)
        o_ref[...] = h_ref[...] + jnp.dot(jnp.concatenate(m_chunks, axis=1), rows, preferred_element_type=F32)

    @pl.when(i % 2 == 0)
    def _():
        reduce_tile(0)

    @pl.when(i % 2 == 1)
    def _():
        reduce_tile(1)


def _combine(h, apos, piece_src, ys, n_experts):
    n, d = h.shape
    tm = min(512, n)
    nt = n // tm
    kloc = _local_rows(tm, n_experts)
    n_pieces = kloc // SEG_ALIGN
    table = lambda imap: pl.BlockSpec((1, 1, n_pieces), imap, memory_space=pltpu.SMEM)
    return pl.pallas_call(
        _combine_kernel,
        grid=(nt,),
        in_specs=[
            table(lambda i: (i, 0, 0)),
            table(lambda i: (jnp.minimum(i + 1, nt - 1), 0, 0)),
            pl.BlockSpec((tm, d), lambda i: (i, 0)),
            pl.BlockSpec((tm, LANES), lambda i: (i, 0)),
            pl.BlockSpec(memory_space=pl.ANY),
        ],
        out_specs=pl.BlockSpec((tm, d), lambda i: (i, 0)),
        out_shape=jax.ShapeDtypeStruct((n, d), F32),
        scratch_shapes=[
            pltpu.VMEM((kloc, d // 2), jnp.uint32),
            pltpu.VMEM((kloc, d // 2), jnp.uint32),
            pltpu.SemaphoreType.DMA((2,)),
        ],
        compiler_params=pltpu.CompilerParams(dimension_semantics=("arbitrary",), vmem_limit_bytes=VMEM_LIMIT_BYTES),
        name="combine",
    )(piece_src, piece_src, h, apos, ys)


EXPERT_TILE = 512


def _layer(h2d, batch, seq, norm_mix, w_in, b_forget, q_norm_fox, k_norm_fox, q_norm_swa, k_norm_swa, sinks, rel_bias,
           w_out, norm_moe, w_router, b_router, w_up_gate, b_up_gate, w_down, b_down):
    n, d = h2d.shape
    n_experts = w_router.shape[1]
    qa, ka, fv, sq, sk, sv = _proj(h2d, norm_mix, w_in, b_forget, q_norm_fox, k_norm_fox, q_norm_swa, k_norm_swa, seq)
    fa = _fox(qa, ka, fv, batch, seq)
    sb = _swa(sq, sk, sv, sinks, rel_bias, batch, seq)
    h, hn, apos, rowst, tab = _outproj(h2d, fa, sb, w_out, norm_moe, w_router, b_router)

    tr = min(EXPERT_TILE, n)
    tab = tab.reshape(-1, 8, LANES)[:, :, :n_experts].astype(jnp.int32)
    base, cnt, ast = tab[:, 0], tab[:, 1], tab[:, 2]
    nch = (cnt + SEG_ALIGN - 1) // SEG_ALIGN
    counts_i = base[-1] + nch[-1] * SEG_ALIGN
    padded = (counts_i + tr - 1) // tr * tr
    pend = jnp.cumsum(padded)
    pstart = pend - padded
    gdst = pstart[None, :] + base
    n_tok_tiles = tab.shape[0]
    kloc = _local_rows(n // n_tok_tiles, n_experts)
    worst_rows = n * TOP_K + n_tok_tiles * n_experts * (SEG_ALIGN - 1) + n_experts * (tr - 1)
    used_tiles = -(-worst_rows // tr)
    spare_start = used_tiles * tr
    n_tiles = used_tiles + -(-kloc // tr)
    rows_total = n_tiles * tr
    blk_start = jnp.arange(n_tiles, dtype=jnp.int32) * tr
    blk_exp = jnp.minimum(jnp.sum(pend[None, :] <= blk_start[:, None], axis=1), n_experts - 1).astype(jnp.int32)
    blk_valid = jnp.clip(pstart[blk_exp] + counts_i[blk_exp] - blk_start, 0, tr).astype(jnp.int32)

    piece = jnp.arange(kloc // SEG_ALIGN, dtype=jnp.int32) * SEG_ALIGN
    aend = ast + nch * SEG_ALIGN
    owner = jnp.sum(aend[:, None, :] <= piece[None, :, None], axis=-1)
    onehot = owner[:, :, None] == jnp.arange(n_experts, dtype=jnp.int32)
    shift = jnp.sum(jnp.where(onehot, (gdst - ast)[:, None, :], 0), axis=-1)
    used = owner < n_experts
    piece_dst = jnp.where(used, shift + piece, spare_start + piece).astype(jnp.int32)[:, None, :]
    piece_src = jnp.where(used, shift + piece, 0).astype(jnp.int32)[:, None, :]

    xs_sorted = _dispatch(hn, rowst, piece_dst, rows_total, n_experts)
    ys = _experts(xs_sorted, blk_exp, blk_valid, w_up_gate, b_up_gate, w_down, b_down, tr)
    return _combine(h, apos, piece_src, ys, n_experts)


def kernel(x, norm_mix, w_in, b_forget, q_norm_fox, k_norm_fox, q_norm_swa, k_norm_swa, sinks, rel_bias, w_out, norm_moe,
           w_router, b_router, w_up_gate, b_up_gate, w_down, b_down):
    batch, seq, d = x.shape
    h = x.reshape(batch * seq, d)
    for l in range(norm_mix.shape[0]):
        h = _layer(h, batch, seq, norm_mix[l], w_in[l], b_forget[l], q_norm_fox[l], k_norm_fox[l], q_norm_swa[l],
                   k_norm_swa[l], sinks[l], rel_bias, w_out[l], norm_moe[l], w_router[l], b_router[l], w_up_gate[l],
                   b_up_gate[l], w_down[l], b_down[l])
    return h.reshape(batch, seq, d)
```

```python
import functools
import math

import numpy as np
import jax
import jax.numpy as jnp
from jax import lax
from jax.experimental import pallas as pl
from jax.experimental.pallas import tpu as pltpu

HEAD_DIM = 64
FOX_HEADS = 8
SWA_HEADS = 8
SWA_KV_HEADS = 2
SWA_GROUP = SWA_HEADS // SWA_KV_HEADS
WINDOW = 128
N_BUCKETS = 32
MAX_DISTANCE = 128
TOP_K = 4
SWIGLU_LIMIT = 7.0
SWIGLU_ALPHA = 1.702
EPS = 1e-6

LANES = 128
VMEM_LIMIT_BYTES = 56 * 1024 * 1024
NEG_INF = float("-inf")
LOG2E = math.log2(math.e)

F32 = jnp.float32
BF16 = jnp.bfloat16


def _lane_iota(shape):
    return lax.broadcasted_iota(jnp.int32, shape, len(shape) - 1)


def _split3_bf16(v):
    hi = v.astype(BF16)
    r1 = v - hi.astype(F32)
    mid = r1.astype(BF16)
    lo = (r1 - mid.astype(F32)).astype(BF16)
    return hi, mid, lo


def _pack_bf16_pairs(v):
    w = v.shape[1] // 2
    u = pltpu.bitcast(v.astype(BF16).astype(F32), jnp.uint32)
    return u[:, :w] | (u[:, w:] >> 16)


def _unpack_bf16_pairs(u):
    first = pltpu.bitcast(u & jnp.uint32(0xFFFF0000), F32)
    second = pltpu.bitcast(u << 16, F32)
    return first, second


CUM_CHUNK = 128


def _proj_kernel(x_ref, g_ref, wqa_ref, wka_ref, wrest_ref, bf_ref, gqa_ref, gka_ref, gsq_ref, gsk_ref, tri_ref,
                 qa_ref, ka_ref, fv_ref, sq_ref, sk_ref, sv_ref, carry_ref, *, tiles_per_seq):
    i = pl.program_id(0)
    tm = x_ref.shape[0]
    nh = FOX_HEADS
    scale = HEAD_DIM ** -0.5

    @pl.when(i % tiles_per_seq == 0)
    def _():
        carry_ref[...] = jnp.zeros_like(carry_ref)

    x = x_ref[...]
    ms = jnp.mean(x * x, axis=-1, keepdims=True)
    hn = (x * lax.rsqrt(ms + EPS) * g_ref[...]).astype(BF16)

    qa = jnp.dot(hn, wqa_ref[...], preferred_element_type=F32)
    ka = jnp.dot(hn, wka_ref[...], preferred_element_type=F32)
    rest = jnp.dot(hn, wrest_ref[...], preferred_element_type=F32)

    lane_w = _lane_iota((tm, nh * LANES)) % LANES
    upper = lane_w >= HEAD_DIM

    fl = qa + bf_ref[...]
    ls = jnp.minimum(fl, 0.0) - jnp.log(1.0 + jnp.exp(-jnp.abs(fl)))
    ls = jnp.where(upper, ls, 0.0)
    tri = tri_ref[...]
    carry = carry_ref[0:1, :]
    c_chunks = []
    for r in range(tm // CUM_CHUNK):
        hi, mid, lo = _split3_bf16(ls[r * CUM_CHUNK:(r + 1) * CUM_CHUNK, :])
        cs = jnp.dot(tri, jnp.concatenate([hi, mid, lo], axis=0), preferred_element_type=F32) + carry
        carry = cs[CUM_CHUNK - 1:CUM_CHUNK, :]
        c_chunks.append(cs)
    carry_ref[0:1, :] = carry
    c = jnp.concatenate(c_chunks, axis=0) * LOG2E

    c1 = c.astype(BF16).astype(F32)
    c2 = (c - c1).astype(BF16).astype(F32)
    c3 = c - c1 - c2

    def head_norm(t, g_row):
        outs = []
        for h in range(t.shape[1] // LANES):
            th = t[:, h * LANES:(h + 1) * LANES]
            lo_mask = _lane_iota(th.shape) < HEAD_DIM
            ss = jnp.sum(jnp.where(lo_mask, th * th, 0.0), axis=-1, keepdims=True)
            outs.append(th * lax.rsqrt(ss * (1.0 / HEAD_DIM) + EPS))
        return jnp.concatenate(outs, axis=1) * g_row

    qn = head_norm(qa, gqa_ref[...]) * (scale * LOG2E)
    kn = head_norm(ka, gka_ref[...])

    one = jnp.float32(1.0)
    zero = jnp.float32(0.0)
    q_aug = jnp.where(lane_w == 64, c1, jnp.where(lane_w == 65, c2, jnp.where(lane_w == 66, c3,
                      jnp.where(lane_w < 70, one, zero))))
    k_aug = jnp.where(lane_w < 67, one, jnp.where(lane_w == 67, -c1, jnp.where(lane_w == 68, -c2,
                      jnp.where(lane_w == 69, -c3, zero))))
    qa_ref[...] = jnp.where(upper, q_aug, qn).astype(BF16)
    ka_ref[...] = jnp.where(upper, k_aug, kn).astype(BF16)

    nfv = FOX_HEADS * HEAD_DIM
    nsq = SWA_HEADS * HEAD_DIM
    nsk = SWA_KV_HEADS * HEAD_DIM
    fv_ref[...] = rest[:, :nfv].astype(BF16)

    def half_norm(t, g_row):
        outs = []
        for j in range(t.shape[1] // LANES):
            tj = t[:, j * LANES:(j + 1) * LANES]
            lo_mask = _lane_iota(tj.shape) < HEAD_DIM
            sq = tj * tj
            s_lo = jnp.sum(jnp.where(lo_mask, sq, 0.0), axis=-1, keepdims=True)
            s_hi = jnp.sum(jnp.where(lo_mask, 0.0, sq), axis=-1, keepdims=True)
            r = jnp.where(lo_mask, lax.rsqrt(s_lo * (1.0 / HEAD_DIM) + EPS), lax.rsqrt(s_hi * (1.0 / HEAD_DIM) + EPS))
            outs.append(tj * r)
        return jnp.concatenate(outs, axis=1) * g_row

    sq_ref[...] = (half_norm(rest[:, nfv:nfv + nsq], gsq_ref[...]) * scale).astype(BF16)
    sk_ref[...] = half_norm(rest[:, nfv + nsq:nfv + nsq + nsk], gsk_ref[...]).astype(BF16)
    sv_ref[...] = rest[:, nfv + nsq + nsk:].astype(BF16)


def _proj(x2d, norm_mix, w_in, b_forget, q_norm_fox, k_norm_fox, q_norm_swa, k_norm_swa, seq):
    n, d = x2d.shape
    nh, dh = FOX_HEADS, HEAD_DIM
    tm = min(512, seq)
    assert seq % tm == 0 and tm % CUM_CHUNK == 0
    points = np.cumsum([nh * dh, nh * dh, nh * dh, nh, SWA_HEADS * dh, SWA_KV_HEADS * dh, SWA_KV_HEADS * dh])
    wfq, wfk, wfv, wff, wsq, wsk, wsv = jnp.split(w_in, [int(p) for p in points[:-1]], axis=1)

    wfq = wfq.reshape(d, nh, dh)
    wff_rep = jnp.broadcast_to(wff[:, :, None], (d, nh, dh))
    wqa = jnp.concatenate([wfq, wff_rep], axis=2).reshape(d, nh * LANES).astype(BF16)
    wka = jnp.concatenate([wfk.reshape(d, nh, dh), jnp.zeros((d, nh, dh), w_in.dtype)], axis=2)
    wka = wka.reshape(d, nh * LANES).astype(BF16)
    wsq = wsq.reshape(d, SWA_KV_HEADS, SWA_GROUP, dh).transpose(0, 2, 1, 3).reshape(d, SWA_HEADS * dh)
    wrest = jnp.concatenate([wfv, wsq, wsk, wsv], axis=1).astype(BF16)

    zeros_h = jnp.zeros((nh, dh), F32)
    bf_row = jnp.concatenate([zeros_h, jnp.broadcast_to(b_forget[:, None], (nh, dh))], axis=1).reshape(1, nh * LANES)
    gqa = jnp.concatenate([jnp.broadcast_to(q_norm_fox[None], (nh, dh)), zeros_h], axis=1).reshape(1, nh * LANES)
    gka = jnp.concatenate([jnp.broadcast_to(k_norm_fox[None], (nh, dh)), zeros_h], axis=1).reshape(1, nh * LANES)
    gsq = jnp.tile(q_norm_swa, SWA_HEADS).reshape(1, SWA_HEADS * dh)
    gsk = jnp.tile(k_norm_swa, SWA_KV_HEADS).reshape(1, SWA_KV_HEADS * dh)
    tri = np.tril(np.ones((CUM_CHUNK, CUM_CHUNK), np.float32))
    tri3 = jnp.asarray(np.concatenate([tri, tri, tri], axis=1), BF16)

    nrest = wrest.shape[1]
    const = lambda i: (0, 0)
    row = lambda i: (i, 0)
    out_shapes = (
        jax.ShapeDtypeStruct((n, nh * LANES), BF16),
        jax.ShapeDtypeStruct((n, nh * LANES), BF16),
        jax.ShapeDtypeStruct((n, nh * dh), BF16),
        jax.ShapeDtypeStruct((n, SWA_HEADS * dh), BF16),
        jax.ShapeDtypeStruct((n, SWA_KV_HEADS * dh), BF16),
        jax.ShapeDtypeStruct((n, SWA_KV_HEADS * dh), BF16),
    )
    return pl.pallas_call(
        functools.partial(_proj_kernel, tiles_per_seq=seq // tm),
        grid=(n // tm,),
        in_specs=[
            pl.BlockSpec((tm, d), row),
            pl.BlockSpec((1, d), const),
            pl.BlockSpec((d, nh * LANES), const),
            pl.BlockSpec((d, nh * LANES), const),
            pl.BlockSpec((d, nrest), const),
            pl.BlockSpec((1, nh * LANES), const),
            pl.BlockSpec((1, nh * LANES), const),
            pl.BlockSpec((1, nh * LANES), const),
            pl.BlockSpec((1, SWA_HEADS * dh), const),
            pl.BlockSpec((1, SWA_KV_HEADS * dh), const),
            pl.BlockSpec((CUM_CHUNK, 3 * CUM_CHUNK), const),
        ],
        out_specs=[
            pl.BlockSpec((tm, nh * LANES), row),
            pl.BlockSpec((tm, nh * LANES), row),
            pl.BlockSpec((tm, nh * dh), row),
            pl.BlockSpec((tm, SWA_HEADS * dh), row),
            pl.BlockSpec((tm, SWA_KV_HEADS * dh), row),
            pl.BlockSpec((tm, SWA_KV_HEADS * dh), row),
        ],
        out_shape=out_shapes,
        scratch_shapes=[pltpu.VMEM((8, nh * LANES), F32)],
        compiler_params=pltpu.CompilerParams(dimension_semantics=("arbitrary",), vmem_limit_bytes=VMEM_LIMIT_BYTES),
        name="proj",
    )(x2d, norm_mix.reshape(1, d), wqa, wka, wrest, bf_row, gqa, gka, gsq, gsk, tri3)


FOX_BLOCK = 512


def _fox_kernel(qa_ref, ka_ref, v_ref, o_ref, s0_sc, s1_sc, m_sc, l_sc, acc_sc, *, blk):
    i = pl.program_id(2)
    nt = (((1,), (1,)), ((), ()))
    row = lax.broadcasted_iota(jnp.int32, (blk, blk), 0)
    col = lax.broadcasted_iota(jnp.int32, (blk, blk), 1)
    causal = col <= row
    s_bufs = (s0_sc, s1_sc)

    def lane_tiles(s):
        return [s[:, c * LANES:(c + 1) * LANES] for c in range(s.shape[1] // LANES)]

    def row_max(tiles):
        m = jnp.max(functools.reduce(jnp.maximum, tiles), axis=-1, keepdims=True)
        return jnp.broadcast_to(m, (blk, LANES))

    def row_sum(tiles):
        t = jnp.sum(functools.reduce(jnp.add, tiles), axis=-1, keepdims=True)
        return jnp.broadcast_to(t, (blk, LANES))

    def produce_scores(j, slot):
        start = pl.multiple_of(j * blk, blk)
        for hh in range(2):
            q = qa_ref[:, hh * LANES:(hh + 1) * LANES]
            k = ka_ref[pl.ds(start, blk), hh * LANES:(hh + 1) * LANES]
            s_bufs[slot][hh] = lax.dot_general(q, k, nt, preferred_element_type=F32)

    def consume_scores(j, slot, masked):
        v = v_ref[pl.ds(pl.multiple_of(j * blk, blk), blk), :]
        for hh in range(2):
            s = s_bufs[slot][hh]
            if masked:
                s = jnp.where(causal, s, NEG_INF)
            tiles = lane_tiles(s)
            m = m_sc[hh]
            m_new = jnp.maximum(m, row_max(tiles))
            alpha = jnp.exp2(m - m_new)
            p_tiles = [jnp.exp2(t - m_new) for t in tiles]
            p = jnp.concatenate(p_tiles, axis=1).astype(BF16)
            m_sc[hh] = m_new
            l_sc[hh] = alpha * l_sc[hh] + row_sum(p_tiles)
            acc_sc[hh] = alpha * acc_sc[hh] + jnp.dot(p, v, preferred_element_type=F32)

    m_sc[...] = jnp.full(m_sc.shape, NEG_INF, F32)
    l_sc[...] = jnp.zeros(l_sc.shape, F32)
    acc_sc[...] = jnp.zeros(acc_sc.shape, F32)
    produce_scores(0, 0)

    def pair(jj, c):
        j = 2 * jj
        produce_scores(j + 1, 1)
        consume_scores(j, 0, masked=False)
        produce_scores(j + 2, 0)
        consume_scores(j + 1, 1, masked=False)
        return c

    lax.fori_loop(0, i // 2, pair, 0)

    @pl.when(i % 2 == 0)
    def _():
        consume_scores(i, 0, masked=True)

    @pl.when(i % 2 == 1)
    def _():
        produce_scores(i, 1)
        consume_scores(i - 1, 0, masked=False)
        consume_scores(i, 1, masked=True)

    out0 = acc_sc[0] / l_sc[0]
    out1 = acc_sc[1] / l_sc[1]
    lane = _lane_iota(out0.shape)
    o_ref[...] = jnp.where(lane < HEAD_DIM, out0, out1).astype(o_ref.dtype)


def _fox(qa, ka, fv, batch, seq):
    nh, dh = FOX_HEADS, HEAD_DIM
    blk = min(FOX_BLOCK, seq)
    assert seq % blk == 0
    qa3 = qa.reshape(batch, seq, nh * LANES)
    ka3 = ka.reshape(batch, seq, nh * LANES)
    fv3 = fv.reshape(batch, seq, nh * dh)
    out = pl.pallas_call(
        functools.partial(_fox_kernel, blk=blk),
        grid=(batch, nh // 2, seq // blk),
        in_specs=[
            pl.BlockSpec((None, blk, 2 * LANES), lambda b, j, i: (b, i, j)),
            pl.BlockSpec((None, seq, 2 * LANES), lambda b, j, i: (b, 0, j)),
            pl.BlockSpec((None, seq, LANES), lambda b, j, i: (b, 0, j)),
        ],
        out_specs=pl.BlockSpec((None, blk, LANES), lambda b, j, i: (b, i, j)),
        out_shape=jax.ShapeDtypeStruct((batch, seq, nh * dh), BF16),
        scratch_shapes=[
            pltpu.VMEM((2, blk, blk), F32),
            pltpu.VMEM((2, blk, blk), F32),
            pltpu.VMEM((2, blk, LANES), F32),
            pltpu.VMEM((2, blk, LANES), F32),
            pltpu.VMEM((2, blk, LANES), F32),
        ],
        compiler_params=pltpu.CompilerParams(dimension_semantics=("arbitrary", "arbitrary", "arbitrary"),
                                             vmem_limit_bytes=VMEM_LIMIT_BYTES),
        name="fox",
    )(qa3, ka3, fv3)
    return out.reshape(batch * seq, nh * dh)


def _t5_bucket_table():
    qi = np.arange(WINDOW)[:, None]
    kj = np.arange(2 * WINDOW)[None, :]
    dist = qi + WINDOW - kj
    n = np.maximum(dist, 0)
    max_exact = N_BUCKETS // 2
    nf = np.maximum(n, 1).astype(np.float64)
    large = max_exact + (np.log(nf / max_exact) / math.log(MAX_DISTANCE / max_exact) * (N_BUCKETS - max_exact)).astype(np.int64)
    large = np.minimum(large, N_BUCKETS - 1)
    bucket = np.where(n < max_exact, n, large)
    valid = (dist >= 0) & (dist < WINDOW)
    return np.where(valid, bucket, -1).astype(np.int32)


def _swa_kernel(rb_ref, sinks_ref, bucket_ref, sq_ref, sk_ref, sv_ref, skp_ref, svp_ref, o_ref, bias_ref, *, blocks_per_tile):
    b = pl.program_id(0)
    i = pl.program_id(1)
    w = WINDOW
    g = SWA_GROUP

    @pl.when((b == 0) & (i == 0))
    def _():
        bucket = bucket_ref[...]
        for h in range(SWA_HEADS):
            acc = jnp.full((w, 2 * w), NEG_INF, F32)
            for bk in range(N_BUCKETS):
                acc = jnp.where(bucket == bk, rb_ref[bk, h], acc)
            kv, gi = divmod(h, g)
            bias_ref[kv, gi * w:(gi + 1) * w, :] = acc

    nt = (((1,), (1,)), ((), ()))
    lane = _lane_iota((w, LANES))
    col = _lane_iota((g * w, 2 * w))
    for cb in range(blocks_per_tile):
        rows = slice(cb * w, (cb + 1) * w)
        if cb == 0:
            kprev, vprev = skp_ref[...], svp_ref[...]
        else:
            kprev, vprev = sk_ref[(cb - 1) * w:cb * w, :], sv_ref[(cb - 1) * w:cb * w, :]
        kcat = jnp.concatenate([kprev, sk_ref[rows, :]], axis=0)
        vcat = jnp.concatenate([vprev, sv_ref[rows, :]], axis=0)
        outs = []
        for kv in range(SWA_KV_HEADS):
            keep = (lane < HEAD_DIM) if kv == 0 else (lane >= HEAD_DIM)
            qs = [jnp.where(keep, sq_ref[rows, gi * LANES:(gi + 1) * LANES], jnp.zeros((), BF16)) for gi in range(g)]
            qstack = jnp.concatenate(qs, axis=0)
            s = lax.dot_general(qstack, kcat, nt, preferred_element_type=F32) + bias_ref[kv]
            if cb == 0:
                s = s + jnp.where(col < w, jnp.where(i == 0, NEG_INF, 0.0), 0.0)
            sink = jnp.concatenate([jnp.full((w, LANES), sinks_ref[kv * g + gi], F32) for gi in range(g)], axis=0)
            s_lo, s_hi = s[:, :w], s[:, w:]
            m = jnp.maximum(jnp.broadcast_to(jnp.max(jnp.maximum(s_lo, s_hi), axis=-1, keepdims=True), sink.shape), sink)
            p_lo, p_hi = jnp.exp(s_lo - m), jnp.exp(s_hi - m)
            denom = jnp.broadcast_to(jnp.sum(p_lo + p_hi, axis=-1, keepdims=True), sink.shape) + jnp.exp(sink - m)
            p = jnp.concatenate([p_lo, p_hi], axis=1).astype(BF16)
            outs.append(jnp.dot(p, vcat, preferred_element_type=F32) / denom)
        for gi in range(g):
            o_ref[rows, gi * LANES:(gi + 1) * LANES] = jnp.where(
                lane < HEAD_DIM, outs[0][gi * w:(gi + 1) * w, :], outs[1][gi * w:(gi + 1) * w, :]).astype(o_ref.dtype)


def _swa(sq, sk, sv, sinks, rel_bias, batch, seq):
    w = WINDOW
    dh = HEAD_DIM
    blocks_per_tile = min(4, seq // w)
    t = blocks_per_tile * w
    assert seq % t == 0
    sq3 = sq.reshape(batch, seq, SWA_HEADS * dh)
    sk3 = sk.reshape(batch, seq, SWA_KV_HEADS * dh)
    sv3 = sv.reshape(batch, seq, SWA_KV_HEADS * dh)
    bucket = jnp.asarray(_t5_bucket_table())
    cur = lambda b, i: (b, i, 0)
    prev = lambda b, i: (b, jnp.maximum(i * blocks_per_tile - 1, 0), 0)
    smem = pl.BlockSpec(memory_space=pltpu.SMEM)
    out = pl.pallas_call(
        functools.partial(_swa_kernel, blocks_per_tile=blocks_per_tile),
        grid=(batch, seq // t),
        in_specs=[
            smem, smem,
            pl.BlockSpec((w, 2 * w), lambda b, i: (0, 0)),
            pl.BlockSpec((None, t, SWA_HEADS * dh), cur),
            pl.BlockSpec((None, t, SWA_KV_HEADS * dh), cur),
            pl.BlockSpec((None, t, SWA_KV_HEADS * dh), cur),
            pl.BlockSpec((None, w, SWA_KV_HEADS * dh), prev),
            pl.BlockSpec((None, w, SWA_KV_HEADS * dh), prev),
        ],
        out_specs=pl.BlockSpec((None, t, SWA_HEADS * dh), cur),
        out_shape=jax.ShapeDtypeStruct((batch, seq, SWA_HEADS * dh), BF16),
        scratch_shapes=[pltpu.VMEM((SWA_KV_HEADS, SWA_GROUP * w, 2 * w), F32)],
        compiler_params=pltpu.CompilerParams(dimension_semantics=("arbitrary", "arbitrary"),
                                             vmem_limit_bytes=VMEM_LIMIT_BYTES),
        name="swa",
    )(rel_bias, sinks, bucket, sq3, sk3, sv3, sk3, sv3)
    return out.reshape(batch * seq, SWA_HEADS * dh)


SEG_ALIGN = 8
SEL_ROWS = 32
PERM_CHUNK = 256
PIECE_UNROLL = 8
OUTPROJ_SUB = 512


def _local_rows(tm, n_experts):
    return -(-(tm * TOP_K + n_experts * SEG_ALIGN) // PERM_CHUNK) * PERM_CHUNK


def _outproj_kernel(x_ref, fa_ref, sb_ref, wa_ref, wb_ref, g_ref, wr_ref, br_ref, lstrict_ref, ustrict_ref,
                    sel_ref, h_ref, hn_ref, apos_ref, rowst_ref, tab_ref, carry_ref, *, n_experts):
    i = pl.program_id(0)
    tm = x_ref.shape[0]
    sub = lstrict_ref.shape[0]

    @pl.when(i == 0)
    def _():
        carry_ref[...] = jnp.zeros_like(carry_ref)

    lane = _lane_iota((sub, LANES))
    lane_f = lane.astype(F32)

    routed = []
    cnt = jnp.zeros((1, LANES), F32)
    for s in range(tm // sub):
        rs = slice(s * sub, (s + 1) * sub)
        h = (x_ref[rs, :] + jnp.dot(fa_ref[rs, :], wa_ref[...], preferred_element_type=F32)
             + jnp.dot(sb_ref[rs, :], wb_ref[...], preferred_element_type=F32))
        h_ref[rs, :] = h
        ms = jnp.mean(h * h, axis=-1, keepdims=True)
        hn = h * lax.rsqrt(ms + EPS) * g_ref[...]
        hn_hi = hn.astype(BF16)
        hn_ref[rs, :] = hn_hi

        hn_lo = (hn - hn_hi.astype(F32)).astype(BF16)
        logits = (jnp.dot(hn_hi, wr_ref[0], preferred_element_type=F32)
                  + jnp.dot(hn_lo, wr_ref[0], preferred_element_type=F32)
                  + jnp.dot(hn_hi, wr_ref[1], preferred_element_type=F32) + br_ref[...])
        l = jnp.where(lane < n_experts, logits, NEG_INF)
        vals, idxs = [], []
        member = jnp.zeros((sub, LANES), F32)
        for _ in range(TOP_K):
            mk = jnp.max(l, axis=-1, keepdims=True)
            ik = jnp.min(jnp.where(l == mk, lane_f, float(LANES)), axis=-1, keepdims=True)
            hit = lane_f == ik
            l = jnp.where(hit, NEG_INF, l)
            member = jnp.where(hit, 1.0, member)
            vals.append(mk)
            idxs.append(ik)
        es = [jnp.exp(v - vals[0]) for v in vals]
        tot = es[0] + es[1] + es[2] + es[3]
        gates = [e / tot for e in es]

        lrank = jnp.dot(lstrict_ref[...], member.astype(BF16), preferred_element_type=F32) + cnt
        cnt = lrank[sub - 1:sub, :] + member[sub - 1:sub, :]
        routed.append((idxs, gates, lrank))

    cnt_al = jnp.floor((cnt + (SEG_ALIGN - 1)) * (1.0 / SEG_ALIGN)) * SEG_ALIGN
    base = carry_ref[0:1, :]
    carry_ref[0:1, :] = base + cnt_al
    astart = jnp.dot(jnp.broadcast_to(cnt_al, (8, LANES)).astype(BF16), ustrict_ref[...],
                     preferred_element_type=F32)[0:1, :]
    tab_ref[...] = jnp.concatenate([base, cnt, astart, jnp.zeros((5, LANES), F32)], axis=0)

    col_parts = []
    for s, (idxs, gates, lrank) in enumerate(routed):
        local = astart + lrank
        apos = [jnp.sum(jnp.where(lane_f == ik, local, 0.0), axis=-1, keepdims=True) for ik in idxs]
        ap = jnp.zeros((sub, LANES), F32)
        cols = jnp.zeros((sub, LANES), F32)
        for k in range(TOP_K):
            ap = jnp.where(lane == k, apos[k], ap)
            hi = jnp.floor(apos[k] * (1.0 / 64.0))
            cols = jnp.where(lane == 2 * k, hi, cols)
            cols = jnp.where(lane == 2 * k + 1, apos[k] - 64.0 * hi, cols)
            g1 = gates[k].astype(BF16).astype(F32)
            g2 = (gates[k] - g1).astype(BF16).astype(F32)
            cols = jnp.where(lane == 2 * TOP_K + 3 * k, g1, cols)
            cols = jnp.where(lane == 2 * TOP_K + 3 * k + 1, g2, cols)
            cols = jnp.where(lane == 2 * TOP_K + 3 * k + 2, gates[k] - g1 - g2, cols)
        apos_ref[s * sub:(s + 1) * sub, :] = ap
        col_parts.append(cols.astype(BF16))
    rowst_ref[...] = lax.dot_general(sel_ref[...], jnp.concatenate(col_parts, axis=0), (((1,), (1,)), ((), ())),
                                     preferred_element_type=F32)


def _outproj(x2d, fa, sb, w_out, norm_moe, w_router, b_router):
    n, d = x2d.shape
    n_experts = w_router.shape[1]
    assert n_experts <= LANES
    tm = min(512, n)
    assert n % tm == 0
    nfa = FOX_HEADS * HEAD_DIM
    wa = w_out[:nfa].astype(BF16)
    wb = w_out[nfa:].reshape(SWA_KV_HEADS, SWA_GROUP, HEAD_DIM, d).transpose(1, 0, 2, 3).reshape(SWA_HEADS * HEAD_DIM, d)
    wb = wb.astype(BF16)
    wr_f = jnp.zeros((d, LANES), F32).at[:, :n_experts].set(w_router)
    wr_hi = wr_f.astype(BF16)
    wr = jnp.stack([wr_hi, (wr_f - wr_hi.astype(F32)).astype(BF16)])
    br = jnp.zeros((1, LANES), F32).at[0, :n_experts].set(b_router)
    sub = min(OUTPROJ_SUB, tm)
    assert tm % sub == 0
    lstrict = jnp.asarray(np.tril(np.ones((sub, sub), np.float32), -1), BF16)
    ustrict = jnp.asarray(np.triu(np.ones((LANES, LANES), np.float32), 1), BF16)
    sel = jnp.asarray(np.eye(SEL_ROWS, LANES, dtype=np.float32), BF16)
    assert 2 * TOP_K + 3 * TOP_K <= SEL_ROWS and _local_rows(tm, n_experts) <= 64 * 256
    const = lambda i: (0, 0)
    row = lambda i: (i, 0)
    return pl.pallas_call(
        functools.partial(_outproj_kernel, n_experts=n_experts),
        grid=(n // tm,),
        in_specs=[
            pl.BlockSpec((tm, d), row),
            pl.BlockSpec((tm, nfa), row),
            pl.BlockSpec((tm, SWA_HEADS * HEAD_DIM), row),
            pl.BlockSpec((nfa, d), const),
            pl.BlockSpec((SWA_HEADS * HEAD_DIM, d), const),
            pl.BlockSpec((1, d), const),
            pl.BlockSpec((2, d, LANES), lambda i: (0, 0, 0)),
            pl.BlockSpec((1, LANES), const),
            pl.BlockSpec((sub, sub), const),
            pl.BlockSpec((LANES, LANES), const),
            pl.BlockSpec((SEL_ROWS, LANES), const),
        ],
        out_specs=[
            pl.BlockSpec((tm, d), row),
            pl.BlockSpec((tm, d), row),
            pl.BlockSpec((tm, LANES), row),
            pl.BlockSpec((SEL_ROWS, tm), lambda i: (0, i)),
            pl.BlockSpec((8, LANES), row),
        ],
        out_shape=(
            jax.ShapeDtypeStruct((n, d), F32),
            jax.ShapeDtypeStruct((n, d), BF16),
            jax.ShapeDtypeStruct((n, LANES), F32),
            jax.ShapeDtypeStruct((SEL_ROWS, n), F32),
            jax.ShapeDtypeStruct((n // tm * 8, LANES), F32),
        ),
        scratch_shapes=[pltpu.VMEM((8, LANES), F32)],
        compiler_params=pltpu.CompilerParams(dimension_semantics=("arbitrary",), vmem_limit_bytes=VMEM_LIMIT_BYTES),
        name="outproj",
    )(x2d, fa, sb, wa, wb, norm_moe.reshape(1, d), wr, br, lstrict, ustrict, sel)


def _piece_copies_start(make_copy, n_pieces):
    assert n_pieces % PIECE_UNROLL == 0

    def body(g, carry):
        for u in range(PIECE_UNROLL):
            make_copy(g * PIECE_UNROLL + u).start(priority=u % 2)
        return carry
    lax.fori_loop(0, n_pieces // PIECE_UNROLL, body, 0)


def _dispatch_kernel(dst_ref, hn_ref, rowst_ref, xs_out, perm, loc2, sems):
    i = pl.program_id(0)
    slot = i % 2
    loc = loc2.at[slot]
    tm = hn_ref.shape[0]
    kloc = loc.shape[0]
    w = hn_ref.shape[1] // 2
    r_iota = lax.broadcasted_iota(jnp.int32, (PERM_CHUNK, tm), 0).astype(F32)

    rows = rowst_ref[...]
    a_rows = [64.0 * rows[2 * k:2 * k + 1, :] + rows[2 * k + 1:2 * k + 2, :] for k in range(TOP_K)]
    g_rows = [rows[2 * TOP_K + 3 * k:2 * TOP_K + 3 * k + 1, :] + rows[2 * TOP_K + 3 * k + 1:2 * TOP_K + 3 * k + 2, :]
              + rows[2 * TOP_K + 3 * k + 2:2 * TOP_K + 3 * k + 3, :] for k in range(TOP_K)]

    for c in range(kloc // PERM_CHUNK):
        hit = jnp.zeros((PERM_CHUNK, tm), F32)
        gw = jnp.zeros((PERM_CHUNK, tm), F32)
        for k in range(TOP_K):
            eq = r_iota == a_rows[k] - float(c * PERM_CHUNK)
            hit = jnp.where(eq, 1.0, hit)
            gw = jnp.where(eq, g_rows[k], gw)
        perm[c * PERM_CHUNK:(c + 1) * PERM_CHUNK, :] = hit.astype(BF16)
        gsum = functools.reduce(jnp.add, [gw[:, j * LANES:(j + 1) * LANES] for j in range(tm // LANES)])
        gate = jnp.broadcast_to(jnp.sum(gsum, axis=-1, keepdims=True), (PERM_CHUNK, LANES))
        loc[c * PERM_CHUNK:(c + 1) * PERM_CHUNK, w:] = pltpu.bitcast(gate, jnp.uint32)

    loc[:, :w] = _pack_bf16_pairs(jnp.dot(perm[...], hn_ref[...], preferred_element_type=F32))

    def piece_copy(p):
        s0 = pl.multiple_of(p * SEG_ALIGN, SEG_ALIGN)
        d0 = pl.multiple_of(dst_ref[0, 0, p], SEG_ALIGN)
        return pltpu.make_async_copy(loc.at[pl.ds(s0, SEG_ALIGN), :], xs_out.at[pl.ds(d0, SEG_ALIGN), :],
                                     sems.at[slot])

    _piece_copies_start(piece_copy, kloc // SEG_ALIGN)

    def drain(s):
        pltpu.make_async_copy(loc2.at[s], xs_out.at[pl.ds(0, kloc), :], sems.at[s]).wait()

    @pl.when(i > 0)
    def _():
        drain(1 - slot)

    @pl.when(i == pl.num_programs(0) - 1)
    def _():
        drain(slot)


def _dispatch(hn, rowst, piece_dst, rows_total, n_experts):
    n, d = hn.shape
    tm = min(512, n)
    kloc = _local_rows(tm, n_experts)
    n_pieces = kloc // SEG_ALIGN
    return pl.pallas_call(
        _dispatch_kernel,
        grid=(n // tm,),
        in_specs=[
            pl.BlockSpec((1, 1, n_pieces), lambda i: (i, 0, 0), memory_space=pltpu.SMEM),
            pl.BlockSpec((tm, d), lambda i: (i, 0)),
            pl.BlockSpec((SEL_ROWS, tm), lambda i: (0, i)),
        ],
        out_specs=pl.BlockSpec(memory_space=pl.ANY),
        out_shape=jax.ShapeDtypeStruct((rows_total, d // 2 + LANES), jnp.uint32),
        scratch_shapes=[
            pltpu.VMEM((kloc, tm), BF16),
            pltpu.VMEM((2, kloc, d // 2 + LANES), jnp.uint32),
            pltpu.SemaphoreType.DMA((2,)),
        ],
        compiler_params=pltpu.CompilerParams(dimension_semantics=("arbitrary",), vmem_limit_bytes=VMEM_LIMIT_BYTES,
                                             has_side_effects=True),
        name="dispatch",
    )(piece_dst, hn, rowst)


def _experts_kernel(be_ref, bv_ref, first_ref, slot_ref, nxt_ref, x_ref, wug_hbm, bug_ref, wd_hbm, bd_ref, y_ref,
                    wug_f32, wd_f32, wug_bf, wd_bf, sems):
    i = pl.program_id(0)
    f = wd_bf.shape[0]

    def weight_copies(expert, slot):
        return (pltpu.make_async_copy(wug_hbm.at[expert], wug_f32.at[slot], sems.at[slot, 0]),
                pltpu.make_async_copy(wd_hbm.at[expert], wd_f32.at[slot], sems.at[slot, 1]))

    @pl.when(i == 0)
    def _():
        for cp in weight_copies(be_ref[0], 0):
            cp.start()

    @pl.when(first_ref[i] == 1)
    def _():
        slot = slot_ref[i]
        for cp in weight_copies(be_ref[i], slot):
            cp.wait()
        wug_bf[...] = wug_f32[slot].astype(BF16)
        wd_bf[...] = wd_f32[slot].astype(BF16)

        @pl.when(nxt_ref[i] >= 0)
        def _():
            for cp in weight_copies(nxt_ref[i], 1 - slot):
                cp.start()

    @pl.when(bv_ref[i] > 0)
    def _():
        w = y_ref.shape[1]
        xw = x_ref[...]
        row = lax.broadcasted_iota(jnp.int32, xw.shape, 0)
        xw = jnp.where(row < bv_ref[i], xw, jnp.uint32(0))
        first, second = _unpack_bf16_pairs(xw[:, :w])
        x = jnp.concatenate([first.astype(BF16), second.astype(BF16)], axis=1)
        gu = jnp.dot(x, wug_bf[...], preferred_element_type=F32) + bug_ref[...]
        gate = jnp.minimum(gu[:, :f], SWIGLU_LIMIT)
        up = jnp.clip(gu[:, f:], -SWIGLU_LIMIT, SWIGLU_LIMIT)
        act = (up + 1.0) * (gate * (1.0 / (1.0 + jnp.exp(-SWIGLU_ALPHA * gate))))
        y = jnp.dot(act.astype(BF16), wd_bf[...], preferred_element_type=F32) + bd_ref[...]
        gate_row = pltpu.bitcast(xw[:, w:], F32)
        y = y * jnp.concatenate([gate_row] * (y.shape[1] // LANES), axis=1)
        y_ref[...] = _pack_bf16_pairs(y)

    @pl.when(bv_ref[i] <= 0)
    def _():
        y_ref[...] = jnp.zeros_like(y_ref)


def _experts(xs_sorted, blk_exp, blk_valid, w_up_gate, b_up_gate, w_down, b_down, tr):
    rows_total, w = xs_sorted.shape
    e, d, f2 = w_up_gate.shape
    f = f2 // 2
    n_tiles = rows_total // tr
    changed = jnp.concatenate([jnp.ones((1,), jnp.int32), (blk_exp[1:] != blk_exp[:-1]).astype(jnp.int32)])
    slot = (jnp.cumsum(changed) - 1) % 2
    tile_ids = jnp.arange(n_tiles, dtype=jnp.int32)
    later_first = (changed[None, :] == 1) & (tile_ids[None, :] > tile_ids[:, None])
    nxt_tile = jnp.min(jnp.where(later_first, tile_ids[None, :], n_tiles), axis=1)
    nxt_exp = jnp.where(nxt_tile < n_tiles, blk_exp[jnp.minimum(nxt_tile, n_tiles - 1)], -1).astype(jnp.int32)
    row_map = lambda i, *_: (i, 0)
    exp_map = lambda i, be, *_: (be[i], 0, 0)
    grid_spec = pltpu.PrefetchScalarGridSpec(
        num_scalar_prefetch=5,
        grid=(n_tiles,),
        in_specs=[
            pl.BlockSpec((tr, w), row_map),
            pl.BlockSpec(memory_space=pl.ANY),
            pl.BlockSpec((None, 1, f2), exp_map),
            pl.BlockSpec(memory_space=pl.ANY),
            pl.BlockSpec((None, 1, d), exp_map),
        ],
        out_specs=pl.BlockSpec((tr, d // 2), row_map),
        scratch_shapes=[
            pltpu.VMEM((2, d, f2), F32), pltpu.VMEM((2, f, d), F32),
            pltpu.VMEM((d, f2), BF16), pltpu.VMEM((f, d), BF16),
            pltpu.SemaphoreType.DMA((2, 2)),
        ],
    )
    return pl.pallas_call(
        _experts_kernel,
        grid_spec=grid_spec,
        out_shape=jax.ShapeDtypeStruct((rows_total, d // 2), jnp.uint32),
        compiler_params=pltpu.CompilerParams(dimension_semantics=("arbitrary",), vmem_limit_bytes=VMEM_LIMIT_BYTES),
        name="experts",
    )(blk_exp, blk_valid, changed, slot.astype(jnp.int32), nxt_exp,
      xs_sorted, w_up_gate, b_up_gate.reshape(e, 1, f2), w_down, b_down.reshape(e, 1, d))


def _combine_kernel(src_ref, srcn_ref, h_ref, apos_ref, ys_ref, o_ref, loc_a, loc_b, sems):
    i = pl.program_id(0)
    n_steps = pl.num_programs(0)
    tm = h_ref.shape[0]
    kloc = loc_a.shape[0]
    bufs = (loc_a, loc_b)

    def fetch(table_ref, slot):
        def piece_copy(p):
            s0 = pl.multiple_of(table_ref[0, 0, p], SEG_ALIGN)
            d0 = pl.multiple_of(p * SEG_ALIGN, SEG_ALIGN)
            return pltpu.make_async_copy(ys_ref.at[pl.ds(s0, SEG_ALIGN), :], bufs[slot].at[pl.ds(d0, SEG_ALIGN), :],
                                         sems.at[slot])
        _piece_copies_start(piece_copy, kloc // SEG_ALIGN)

    @pl.when(i == 0)
    def _():
        fetch(src_ref, 0)

    def reduce_tile(slot):
        @pl.when(i + 1 < n_steps)
        def _():
            fetch(srcn_ref, 1 - slot)

        pltpu.make_async_copy(ys_ref.at[pl.ds(0, kloc), :], bufs[slot], sems.at[slot]).wait()
        ap = apos_ref[...]
        a_cols = [ap[:, k:k + 1] for k in range(TOP_K)]
        m_chunks = []
        for c in range(kloc // PERM_CHUNK):
            r_iota = (lax.broadcasted_iota(jnp.int32, (tm, PERM_CHUNK), 1) + c * PERM_CHUNK).astype(F32)
            hit = jnp.zeros((tm, PERM_CHUNK), F32)
            for k in range(TOP_K):
                hit = jnp.where(r_iota == a_cols[k], 1.0, hit)
            m_chunks.append(hit.astype(BF16))
        first, second = _unpack_bf16_pairs(bufs[slot][...])
        rows = jnp.concatenate([first.astype(BF16), second.astype(BF16)], axis=1)
        o_ref[...] = h_ref[...] + jnp.dot(jnp.concatenate(m_chunks, axis=1), rows, preferred_element_type=F32)

    @pl.when(i % 2 == 0)
    def _():
        reduce_tile(0)

    @pl.when(i % 2 == 1)
    def _():
        reduce_tile(1)


def _combine(h, apos, piece_src, ys, n_experts):
    n, d = h.shape
    tm = min(512, n)
    nt = n // tm
    kloc = _local_rows(tm, n_experts)
    n_pieces = kloc // SEG_ALIGN
    table = lambda imap: pl.BlockSpec((1, 1, n_pieces), imap, memory_space=pltpu.SMEM)
    return pl.pallas_call(
        _combine_kernel,
        grid=(nt,),
        in_specs=[
            table(lambda i: (i, 0, 0)),
            table(lambda i: (jnp.minimum(i + 1, nt - 1), 0, 0)),
            pl.BlockSpec((tm, d), lambda i: (i, 0)),
            pl.BlockSpec((tm, LANES), lambda i: (i, 0)),
            pl.BlockSpec(memory_space=pl.ANY),
        ],
        out_specs=pl.BlockSpec((tm, d), lambda i: (i, 0)),
        out_shape=jax.ShapeDtypeStruct((n, d), F32),
        scratch_shapes=[
            pltpu.VMEM((kloc, d // 2), jnp.uint32),
            pltpu.VMEM((kloc, d // 2), jnp.uint32),
            pltpu.SemaphoreType.DMA((2,)),
        ],
        compiler_params=pltpu.CompilerParams(dimension_semantics=("arbitrary",), vmem_limit_bytes=VMEM_LIMIT_BYTES),
        name="combine",
    )(piece_src, piece_src, h, apos, ys)


EXPERT_TILE = 512


def _layer(h2d, batch, seq, norm_mix, w_in, b_forget, q_norm_fox, k_norm_fox, q_norm_swa, k_norm_swa, sinks, rel_bias,
           w_out, norm_moe, w_router, b_router, w_up_gate, b_up_gate, w_down, b_down):
    n, d = h2d.shape
    n_experts = w_router.shape[1]
    qa, ka, fv, sq, sk, sv = _proj(h2d, norm_mix, w_in, b_forget, q_norm_fox, k_norm_fox, q_norm_swa, k_norm_swa, seq)
    fa = _fox(qa, ka, fv, batch, seq)
    sb = _swa(sq, sk, sv, sinks, rel_bias, batch, seq)
    h, hn, apos, rowst, tab = _outproj(h2d, fa, sb, w_out, norm_moe, w_router, b_router)

    tr = min(EXPERT_TILE, n)
    tab = tab.reshape(-1, 8, LANES)[:, :, :n_experts].astype(jnp.int32)
    base, cnt, ast = tab[:, 0], tab[:, 1], tab[:, 2]
    nch = (cnt + SEG_ALIGN - 1) // SEG_ALIGN
    counts_i = base[-1] + nch[-1] * SEG_ALIGN
    padded = (counts_i + tr - 1) // tr * tr
    pend = jnp.cumsum(padded)
    pstart = pend - padded
    gdst = pstart[None, :] + base
    n_tok_tiles = tab.shape[0]
    kloc = _local_rows(n // n_tok_tiles, n_experts)
    worst_rows = n * TOP_K + n_tok_tiles * n_experts * (SEG_ALIGN - 1) + n_experts * (tr - 1)
    used_tiles = -(-worst_rows // tr)
    spare_start = used_tiles * tr
    n_tiles = used_tiles + -(-2 * kloc // tr)
    rows_total = n_tiles * tr
    blk_start = jnp.arange(n_tiles, dtype=jnp.int32) * tr
    blk_exp = jnp.minimum(jnp.sum(pend[None, :] <= blk_start[:, None], axis=1), n_experts - 1).astype(jnp.int32)
    blk_valid = jnp.clip(pstart[blk_exp] + counts_i[blk_exp] - blk_start, 0, tr).astype(jnp.int32)

    piece = jnp.arange(kloc // SEG_ALIGN, dtype=jnp.int32) * SEG_ALIGN
    aend = ast + nch * SEG_ALIGN
    owner = jnp.sum(aend[:, None, :] <= piece[None, :, None], axis=-1)
    onehot = owner[:, :, None] == jnp.arange(n_experts, dtype=jnp.int32)
    shift = jnp.sum(jnp.where(onehot, (gdst - ast)[:, None, :], 0), axis=-1)
    used = owner < n_experts
    spare = spare_start + (jnp.arange(n_tok_tiles, dtype=jnp.int32) % 2)[:, None] * kloc + piece[None, :]
    piece_dst = jnp.where(used, shift + piece, spare).astype(jnp.int32)[:, None, :]
    piece_src = jnp.where(used, shift + piece, 0).astype(jnp.int32)[:, None, :]

    xs_sorted = _dispatch(hn, rowst, piece_dst, rows_total, n_experts)
    ys = _experts(xs_sorted, blk_exp, blk_valid, w_up_gate, b_up_gate, w_down, b_down, tr)
    return _combine(h, apos, piece_src, ys, n_experts)


def kernel(x, norm_mix, w_in, b_forget, q_norm_fox, k_norm_fox, q_norm_swa, k_norm_swa, sinks, rel_bias, w_out, norm_moe,
           w_router, b_router, w_up_gate, b_up_gate, w_down, b_down):
    batch, seq, d = x.shape
    h = x.reshape(batch * seq, d)
    for l in range(norm_mix.shape[0]):
        h = _layer(h, batch, seq, norm_mix[l], w_in[l], b_forget[l], q_norm_fox[l], k_norm_fox[l], q_norm_swa[l],
                   k_norm_swa[l], sinks[l], rel_bias, w_out[l], norm_moe[l], w_router[l], b_router[l], w_up_gate[l],
                   b_up_gate[l], w_down[l], b_down[l])
    return h.reshape(batch, seq, d)
```
